```python
import math, functools
import jax, jax.numpy as jnp
from jax import lax
import numpy as np

D_MODEL = 1024
BATCH = 8
SEQ = 4096
DEPTH = 2
DEC_BATCH = 32
DEC_SEQ = 4
PAST_LEN = 16384
PAGE_SIZE = 128

GROUP_W = D_MODEL // 4
CONV_A_W = 3
N_HEADS_B = 4
HEAD_DIM_B = GROUP_W // N_HEADS_B
CONV_B_W = 4
N_HEADS_C = 4
HEAD_DIM_C = GROUP_W // N_HEADS_C
N_HEADS_D = 4
HEAD_DIM_DV = GROUP_W // N_HEADS_D
HEAD_DIM_DQK = HEAD_DIM_DV // 2
CHUNK = 64
Q_BLOCK = 128
N_BUCKETS = 32
MAX_EXACT = N_BUCKETS // 2
MAX_DISTANCE = 128
D_FF = 2816
N_EXPERTS = 8
TOP_K = 2
D_FF_EXPERT = 1408
N_DENSE = (DEPTH + 1) // 2
N_MOE = DEPTH // 2
ALPHA = (2 * DEPTH) ** 0.25
BETA_INIT = (8 * DEPTH) ** -0.25
LN_EPS = 1e-5
RMS_EPS = 1e-6
SPLITS = (GROUP_W, GROUP_W, GROUP_W,
          3 * GROUP_W, N_HEADS_B, N_HEADS_B, GROUP_W,
          GROUP_W, GROUP_W, GROUP_W, GROUP_W,
          GROUP_W, GROUP_W, GROUP_W)
D_IN = sum(SPLITS)

kernel_name = 'hybrid_parallel_groups_decode_step'


def _split(u, sizes):
    out, o = [], 0
    for s in sizes:
        out.append(u[..., o:o + s])
        o += s
    return out


def _layer_norm(x, g, b):
    xf = x.astype(jnp.float32)
    mu = jnp.mean(xf, -1, keepdims=True)
    var = jnp.mean(jnp.square(xf - mu), -1, keepdims=True)
    return ((xf - mu) * lax.rsqrt(var + LN_EPS) * g + b).astype(x.dtype)


def _rms_norm(x, w):
    xf = x.astype(jnp.float32)
    r = xf * lax.rsqrt(jnp.mean(jnp.square(xf), -1, keepdims=True) + RMS_EPS)
    return (r * w).astype(x.dtype)


def _l2norm(x):
    xf = x.astype(jnp.float32)
    return (xf * lax.rsqrt(jnp.sum(jnp.square(xf), -1, keepdims=True) + RMS_EPS)).astype(x.dtype)


def _causal_dwconv(x, buf, w):
    width, L = w.shape[0], x.shape[1]
    xp = jnp.concatenate([buf.astype(x.dtype), x], axis=1)
    y = sum(xp[:, j:j + L] * w[j] for j in range(width))
    return y, xp[:, xp.shape[1] - (width - 1):]


def _to_chunks(t, n):
    t = t.reshape((t.shape[0], n, CHUNK) + t.shape[2:])
    return jnp.swapaxes(jnp.moveaxis(t, 1, 0), 2, 3)


def _from_chunks(t, L):
    t = jnp.moveaxis(jnp.swapaxes(t, 2, 3), 0, 1)
    return t.reshape((t.shape[0], t.shape[1] * t.shape[2]) + t.shape[3:])[:, :L]


def _pad_f32(t, pad):
    return jnp.pad(t.astype(jnp.float32), [(0, 0), (0, pad)] + [(0, 0)] * (t.ndim - 2))


def _gated_delta(q, k, v, beta, g, s0):
    L, dv = q.shape[1], v.shape[-1]
    n = -(-L // CHUNK)
    pad = n * CHUNK - L
    qc, kc, vc, bc, gc = [_to_chunks(_pad_f32(t, pad), n) for t in (q, k, v, beta, g)]
    incl = jnp.tril(jnp.ones((CHUNK, CHUNK), bool))
    strict = jnp.tril(jnp.ones((CHUNK, CHUNK), bool), -1)

    def step(S, inp):
        qi, ki, vi, bi, gi = inp
        gcum = jnp.cumsum(gi, axis=-1)
        decay = jnp.exp(jnp.where(incl, gcum[..., :, None] - gcum[..., None, :], -jnp.inf))
        a = jnp.where(strict, jnp.einsum('bhik,bhjk->bhij', ki, ki) * decay * bi[..., None], 0.0)
        rhs = jnp.concatenate([vi * bi[..., None], ki * (bi * jnp.exp(gcum))[..., None]], axis=-1)
        sol = lax.linalg.triangular_solve(a, rhs, left_side=True, lower=True, unit_diagonal=True)
        u, w = sol[..., :dv], sol[..., dv:]
        v_new = u - jnp.einsum('bhck,bhkv->bhcv', w, S)
        o = (jnp.einsum('bhck,bhkv->bhcv', qi * jnp.exp(gcum)[..., None], S)
             + jnp.einsum('bhij,bhjv->bhiv', jnp.einsum('bhik,bhjk->bhij', qi, ki) * decay, v_new))
        glast = gcum[..., -1:]
        S = S * jnp.exp(glast)[..., None] + jnp.einsum('bhck,bhcv->bhkv', ki * jnp.exp(glast - gcum)[..., None], v_new)
        return S, o

    S, o = lax.scan(step, s0.astype(jnp.float32), (qc, kc, vc, bc, gc))
    return _from_chunks(o, L).astype(q.dtype), S.astype(s0.dtype)


def _gla(q, k, v, log_f, s0):
    L = q.shape[1]
    n = -(-L // CHUNK)
    pad = n * CHUNK - L
    qc, kc, vc, lc = [_to_chunks(_pad_f32(t, pad), n) for t in (q, k, v, log_f)]
    incl = jnp.tril(jnp.ones((CHUNK, CHUNK), bool))[:, :, None]

    def step(S, inp):
        qi, ki, vi, li = inp
        b = jnp.cumsum(li, axis=2)
        dec = jnp.exp(jnp.where(incl, b[:, :, :, None, :] - b[:, :, None, :, :], -jnp.inf))
        att = jnp.einsum('bhik,bhjk,bhijk->bhij', qi, ki, dec)
        o = jnp.einsum('bhik,bhkv->bhiv', qi * jnp.exp(b), S) + jnp.einsum('bhij,bhjv->bhiv', att, vi)
        blast = b[:, :, -1:, :]
        S = jnp.exp(blast[:, :, 0, :])[..., None] * S + jnp.einsum('bhjk,bhjv->bhkv', ki * jnp.exp(blast - b), vi)
        return S, o

    S, o = lax.scan(step, s0.astype(jnp.float32), (qc, kc, vc, lc))
    return _from_chunks(o, L).astype(q.dtype), S.astype(s0.dtype)


def _t5_bucket(q_pos, k_pos):
    n = jnp.maximum(q_pos[:, None] - k_pos[None, :], 0)
    nf = jnp.maximum(n, MAX_EXACT).astype(jnp.float32)
    large = MAX_EXACT + (jnp.log(nf / MAX_EXACT) / math.log(MAX_DISTANCE / MAX_EXACT)
                         * (N_BUCKETS - MAX_EXACT)).astype(jnp.int32)
    large = jnp.minimum(large, N_BUCKETS - 1)
    return jnp.where(n < MAX_EXACT, n, large)


def _diff_attn_core(q, k, v, q_pos, k_pos, lam, rel_bias):
    s = jnp.einsum('bqhmd,bkhmd->bmhqk', q, k).astype(jnp.float32) * (HEAD_DIM_DQK ** -0.5)
    bias = jnp.transpose(rel_bias[_t5_bucket(q_pos, k_pos)], (2, 0, 1)).astype(jnp.float32)
    s = jnp.where(k_pos[None, :] <= q_pos[:, None], s + bias, -jnp.inf)
    p = jax.nn.softmax(s, axis=-1)
    w = p[:, 0] - lam * p[:, 1]
    return jnp.einsum('bhqk,bkhv->bqhv', w.astype(v.dtype), v)


def _attend_prompt(q, k, v, lam, rel_bias):
    bsz, S = q.shape[:2]
    k_pos = jnp.arange(S)

    def one(i):
        start = i * Q_BLOCK
        qb = lax.dynamic_slice_in_dim(q, start, Q_BLOCK, axis=1)
        return _diff_attn_core(qb, k, v, start + jnp.arange(Q_BLOCK), k_pos, lam, rel_bias)

    o = lax.map(one, jnp.arange(S // Q_BLOCK))
    return jnp.moveaxis(o, 0, 1).reshape(bsz, S, N_HEADS_D, HEAD_DIM_DV)


def _attend_sample(q, k, v, lam, rel_bias, k_pages, v_pages, page_table):
    bsz, L = q.shape[:2]
    past = page_table.shape[1] * k_pages.shape[1]
    kp = k_pages[page_table].reshape(bsz, past, N_HEADS_D, 2, HEAD_DIM_DQK)
    vp = v_pages[page_table].reshape(bsz, past, N_HEADS_D, HEAD_DIM_DV)
    k_all = jnp.concatenate([kp.astype(k.dtype), k], axis=1)
    v_all = jnp.concatenate([vp.astype(v.dtype), v], axis=1)
    return _diff_attn_core(q, k_all, v_all, past + jnp.arange(L), jnp.arange(past + L), lam, rel_bias)


def _swiglu(x, wg, wu, wd):
    return (jax.nn.silu(x @ wg) * (x @ wu)) @ wd


def _moe(x, w_r, w_g, w_u, w_d):
    shp = x.shape
    xt = x.reshape(-1, shp[-1])
    logits = (xt @ w_r).astype(jnp.float32)
    top_v, top_i = lax.top_k(logits, TOP_K)
    gates = jax.nn.softmax(top_v, axis=-1)
    comb = jnp.sum(jax.nn.one_hot(top_i, N_EXPERTS, dtype=jnp.float32) * gates[..., None], axis=1)
    out = jnp.zeros_like(xt)
    for e in range(N_EXPERTS):
        out = out + (comb[:, e:e + 1] * _swiglu(xt, w_g[e], w_u[e], w_d[e])).astype(xt.dtype)
    return out.reshape(shp)


def _layer(l, x, conv_a0, conv_b0, s_gdn0, s_hgrn0, attend, p):
    f32 = jnp.float32
    bsz, L, _ = x.shape
    u = x @ p['w_in'][l]
    (a_in, a_gb, a_gc, b_qkv, b_a, b_b, b_z,
     c_q, c_f, c_i, c_g, d_q, d_k, d_v) = _split(u, SPLITS)
    a_conv, conv_a1 = _causal_dwconv(a_gc * a_in, conv_a0, p['conv_a'][l])
    y_a = a_gb * a_conv
    b_conv, conv_b1 = _causal_dwconv(b_qkv, conv_b0, p['conv_b'][l])
    b_conv = jax.nn.silu(b_conv)
    hb = lambda t: t.reshape(bsz, L, N_HEADS_B, HEAD_DIM_B)
    qb = _l2norm(hb(b_conv[..., :GROUP_W])) * (HEAD_DIM_B ** -0.5)
    kb = _l2norm(hb(b_conv[..., GROUP_W:2 * GROUP_W]))
    vb = hb(b_conv[..., 2 * GROUP_W:])
    beta = jax.nn.sigmoid(b_b.astype(f32))
    g = -jnp.exp(p['gdn_a_log'][l].astype(f32)) * jax.nn.softplus(b_a.astype(f32) + p['gdn_dt_bias'][l].astype(f32))
    ob, s_gdn1 = _gated_delta(qb, kb, vb, beta, g, s_gdn0)
    y_b = (_rms_norm(ob, p['norm_b'][l]) * jax.nn.silu(hb(b_z))).reshape(bsz, L, GROUP_W)
    lbs = jax.nn.softmax(p['lower_bounds'].astype(f32), axis=0)
    lb = (jnp.cumsum(lbs, axis=0) - lbs[0])[l]
    cf = c_f.astype(f32)
    sig_f = jax.nn.sigmoid(cf)
    log_f = jnp.log(lb + (1.0 - lb) * sig_f)
    k_c = (1.0 - lb) * (1.0 - sig_f)
    hc = lambda t: t.reshape(bsz, L, N_HEADS_C, HEAD_DIM_C)
    oc, s_hgrn1 = _gla(hc(jax.nn.silu(c_q)), hc(k_c), hc(c_i), hc(log_f), s_hgrn0)
    y_c = (_rms_norm(oc, p['norm_c'][l]) * jax.nn.silu(hc(c_g))).reshape(bsz, L, GROUP_W)
    qd = d_q.reshape(bsz, L, N_HEADS_D, 2, HEAD_DIM_DQK)
    kd = d_k.reshape(bsz, L, N_HEADS_D, 2, HEAD_DIM_DQK)
    vd = d_v.reshape(bsz, L, N_HEADS_D, HEAD_DIM_DV)
    lam_init = 0.8 - 0.6 * math.exp(-0.3 * l)
    lam = (jnp.exp(jnp.sum(p['lambda_q1'][l].astype(f32) * p['lambda_k1'][l].astype(f32)))
           - jnp.exp(jnp.sum(p['lambda_q2'][l].astype(f32) * p['lambda_k2'][l].astype(f32))) + lam_init)
    od = attend(qd, kd, vd, lam)
    y_d = (_rms_norm(od, p['norm_d'][l]) * (1.0 - lam_init)).reshape(bsz, L, GROUP_W)
    mix = jnp.concatenate([y_a, y_b, y_c, y_d], axis=-1) @ p['w_o'][l]
    x = _layer_norm(ALPHA * x + mix, p['ln1_g'][l], p['ln1_b'][l])
    j = l // 2
    if l % 2 == 0:
        f = _swiglu(x, p['ffn_w_gate'][j], p['ffn_w_up'][j], p['ffn_w_down'][j])
    else:
        f = _moe(x, p['router_w'][j], p['moe_w_gate'][j], p['moe_w_up'][j], p['moe_w_down'][j])
    x = _layer_norm(ALPHA * x + f, p['ln2_g'][l], p['ln2_b'][l])
    k_rows = kd.reshape(bsz, L, N_HEADS_D, 2 * HEAD_DIM_DQK)
    return x, (k_rows, vd, conv_a1, conv_b1, s_gdn1, s_hgrn1)


def _run(x, init, attend, p):
    states = []
    for l in range(DEPTH):
        x, st = _layer(l, x, *init[l], functools.partial(attend, l), p)
        states.append(st)
    return x, [jnp.stack([s[i] for s in states]) for i in range(6)]


def setup_inputs(seed: int = 0) -> dict:
    key = jax.random.key(seed)
    keys = jax.random.split(key, 48)
    ctr = [0]

    def nk():
        ctr[0] += 1
        return keys[ctr[0] - 1]

    f32 = jnp.float32

    def nrm(shape, scale):
        return jax.random.normal(nk(), shape, f32) * scale

    n_pages = PAST_LEN // PAGE_SIZE
    n_pool = (DEC_BATCH * n_pages * 5) // 4
    page_table = jax.random.permutation(nk(), n_pool)[:DEC_BATCH * n_pages].reshape(DEC_BATCH, n_pages).astype(jnp.int32)
    dt = jnp.exp(jax.random.uniform(nk(), (DEPTH, N_HEADS_B), f32, math.log(1e-3), math.log(1e-1)))
    return {
        'x_prompt': nrm((BATCH, SEQ, D_MODEL), 1.0),
        'x_sample': nrm((DEC_BATCH, DEC_SEQ, D_MODEL), 1.0),
        'cache_k': nrm((DEPTH, n_pool, PAGE_SIZE, N_HEADS_D, 2 * HEAD_DIM_DQK), 1.0),
        'cache_v': nrm((DEPTH, n_pool, PAGE_SIZE, N_HEADS_D, HEAD_DIM_DV), 1.0),
        'state_conv_a': nrm((DEPTH, DEC_BATCH, CONV_A_W - 1, GROUP_W), 1.0),
        'state_conv_b': nrm((DEPTH, DEC_BATCH, CONV_B_W - 1, 3 * GROUP_W), 1.0),
        'state_gdn': nrm((DEPTH, DEC_BATCH, N_HEADS_B, HEAD_DIM_B, HEAD_DIM_B), 0.5),
        'state_hgrn': nrm((DEPTH, DEC_BATCH, N_HEADS_C, HEAD_DIM_C, HEAD_DIM_C), 1.0),
        'page_table': page_table,
        'w_in': nrm((DEPTH, D_MODEL, D_IN), D_MODEL ** -0.5),
        'conv_a': nrm((DEPTH, CONV_A_W, GROUP_W), CONV_A_W ** -0.5),
        'conv_b': nrm((DEPTH, CONV_B_W, 3 * GROUP_W), CONV_B_W ** -0.5),
        'gdn_a_log': jnp.log(jax.random.uniform(nk(), (DEPTH, N_HEADS_B), f32, 1.0, 16.0)),
        'gdn_dt_bias': dt + jnp.log(-jnp.expm1(-dt)),
        'norm_b': 1.0 + nrm((DEPTH, HEAD_DIM_B), 0.02),
        'lower_bounds': nrm((DEPTH, GROUP_W), 0.1),
        'norm_c': 1.0 + nrm((DEPTH, HEAD_DIM_C), 0.02),
        'lambda_q1': nrm((DEPTH, HEAD_DIM_DQK), 0.1),
        'lambda_k1': nrm((DEPTH, HEAD_DIM_DQK), 0.1),
        'lambda_q2': nrm((DEPTH, HEAD_DIM_DQK), 0.1),
        'lambda_k2': nrm((DEPTH, HEAD_DIM_DQK), 0.1),
        'norm_d': 1.0 + nrm((DEPTH, HEAD_DIM_DV), 0.02),
        'rel_bias': nrm((N_BUCKETS, N_HEADS_D), 0.5),
        'w_o': nrm((DEPTH, D_MODEL, D_MODEL), D_MODEL ** -0.5 * BETA_INIT),
        'ln1_g': 1.0 + nrm((DEPTH, D_MODEL), 0.02),
        'ln1_b': nrm((DEPTH, D_MODEL), 0.02),
        'ffn_w_gate': nrm((N_DENSE, D_MODEL, D_FF), D_MODEL ** -0.5),
        'ffn_w_up': nrm((N_DENSE, D_MODEL, D_FF), D_MODEL ** -0.5),
        'ffn_w_down': nrm((N_DENSE, D_FF, D_MODEL), D_FF ** -0.5 * BETA_INIT),
        'router_w': nrm((N_MOE, D_MODEL, N_EXPERTS), D_MODEL ** -0.5),
        'moe_w_gate': nrm((N_MOE, N_EXPERTS, D_MODEL, D_FF_EXPERT), D_MODEL ** -0.5),
        'moe_w_up': nrm((N_MOE, N_EXPERTS, D_MODEL, D_FF_EXPERT), D_MODEL ** -0.5),
        'moe_w_down': nrm((N_MOE, N_EXPERTS, D_FF_EXPERT, D_MODEL), D_FF_EXPERT ** -0.5 * BETA_INIT),
        'ln2_g': 1.0 + nrm((DEPTH, D_MODEL), 0.02),
        'ln2_b': nrm((DEPTH, D_MODEL), 0.02),
    }


def reference(x_prompt, x_sample, cache_k, cache_v, state_conv_a, state_conv_b, state_gdn, state_hgrn,
              page_table, w_in, conv_a, conv_b, gdn_a_log, gdn_dt_bias, norm_b, lower_bounds, norm_c,
              lambda_q1, lambda_k1, lambda_q2, lambda_k2, norm_d, rel_bias, w_o, ln1_g, ln1_b,
              ffn_w_gate, ffn_w_up, ffn_w_down, router_w, moe_w_gate, moe_w_up, moe_w_down, ln2_g, ln2_b):
    p = dict(w_in=w_in, conv_a=conv_a, conv_b=conv_b, gdn_a_log=gdn_a_log, gdn_dt_bias=gdn_dt_bias,
             norm_b=norm_b, lower_bounds=lower_bounds, norm_c=norm_c, lambda_q1=lambda_q1,
             lambda_k1=lambda_k1, lambda_q2=lambda_q2, lambda_k2=lambda_k2, norm_d=norm_d, w_o=w_o,
             ln1_g=ln1_g, ln1_b=ln1_b, ffn_w_gate=ffn_w_gate, ffn_w_up=ffn_w_up, ffn_w_down=ffn_w_down,
             router_w=router_w, moe_w_gate=moe_w_gate, moe_w_up=moe_w_up, moe_w_down=moe_w_down,
             ln2_g=ln2_g, ln2_b=ln2_b)
    bp = x_prompt.shape[0]
    dtp = x_prompt.dtype
    init_p = [(jnp.zeros((bp, CONV_A_W - 1, GROUP_W), dtp),
               jnp.zeros((bp, CONV_B_W - 1, 3 * GROUP_W), dtp),
               jnp.zeros((bp, N_HEADS_B, HEAD_DIM_B, HEAD_DIM_B), dtp),
               jnp.zeros((bp, N_HEADS_C, HEAD_DIM_C, HEAD_DIM_C), dtp)) for _ in range(DEPTH)]
    init_s = [(state_conv_a[l], state_conv_b[l], state_gdn[l], state_hgrn[l]) for l in range(DEPTH)]
    attend_p = lambda l, q, k, v, lam: _attend_prompt(q, k, v, lam, rel_bias)
    attend_s = lambda l, q, k, v, lam: _attend_sample(q, k, v, lam, rel_bias, cache_k[l], cache_v[l], page_table)
    y_prompt, (k_p, v_p, ca_p, cb_p, sg_p, sh_p) = _run(x_prompt, init_p, attend_p, p)
    y_sample, (k_s, v_s, ca_s, cb_s, sg_s, sh_s) = _run(x_sample, init_s, attend_s, p)
    return (y_prompt, y_sample, k_p, v_p, k_s, v_s, ca_p, ca_s, cb_p, cb_s, sg_p, sg_s, sh_p, sh_s)
```

```python
import functools
import math

import numpy as np
import jax
import jax.numpy as jnp
from jax import lax
from jax.experimental import pallas as pl
from jax.experimental.pallas import tpu as pltpu

F32 = jnp.float32
BF16 = jnp.bfloat16

D_MODEL = 1024
GROUP_W = 256
N_HEADS = 4
HEAD_DIM = 64
CHUNK = 64
HIST = 8
CONV_A_W = 3
CONV_B_W = 4
HEAD_DIM_DQK = 32
N_BUCKETS = 32
MAX_EXACT = 16
MAX_DISTANCE = 128
N_EXPERTS = 8
LN_EPS = 1e-5
RMS_EPS = 1e-6
NEG = -1e30
U_W = 4096
VMEM_LIMIT = 56 * 1024 * 1024

C_AIN, C_AGB, C_AGC, C_BZ = 0, 256, 512, 768
C_BQKV, C_BA = 1024, 1792
C_CQ, C_CF, C_CI, C_CG = 2048, 2304, 2560, 2816
C_BB, C_DQ, C_DK, C_DV = 3072, 3328, 3584, 3840


def _dot(a, b):
    return jnp.dot(a, b, preferred_element_type=F32)


def _dot_nt(a, b):
    return lax.dot_general(a, b, (((1,), (1,)), ((), ())), preferred_element_type=F32)


def _dot_tn(a, b):
    return lax.dot_general(a, b, (((0,), (0,)), ((), ())), preferred_element_type=F32)


def _split2(x):
    hi = x.astype(BF16)
    lo = (x - hi.astype(F32)).astype(BF16)
    return hi, lo


def _split3(x):
    hi = x.astype(BF16)
    r = x - hi.astype(F32)
    mid = r.astype(BF16)
    lo = (r - mid.astype(F32)).astype(BF16)
    return hi, mid, lo


def _dot_c3(c_bf, x):
    hi, mid, lo = _split3(x)
    return _dot(c_bf, hi) + _dot(c_bf, mid) + _dot(c_bf, lo)


def _segsum(s, bd_bf):
    hi, lo = _split2(s)
    return _dot(hi, bd_bf) + _dot(lo, bd_bf)


def _silu(x):
    return x * jax.nn.sigmoid(x)


def _softplus(x):
    return jnp.maximum(x, 0.0) + jnp.log1p(jnp.exp(-jnp.abs(x)))


def _layer_norm(z, g, b):
    mu = jnp.mean(z, axis=-1, keepdims=True)
    zc = z - mu
    var = jnp.mean(zc * zc, axis=-1, keepdims=True)
    return zc * lax.rsqrt(var + LN_EPS) * g + b


def _proj_kernel(x_ref, w_ref, o_ref):
    o_ref[...] = _dot(x_ref[...].astype(BF16), w_ref[...])


def _proj(x, w, tm, tn):
    n, k = x.shape
    m = w.shape[1]
    return pl.pallas_call(
        _proj_kernel,
        out_shape=jax.ShapeDtypeStruct((n, m), F32),
        grid=(n // tm, m // tn),
        in_specs=[pl.BlockSpec((tm, k), lambda i, j: (i, 0)),
                  pl.BlockSpec((k, tn), lambda i, j: (0, j))],
        out_specs=pl.BlockSpec((tm, tn), lambda i, j: (i, j)),
        compiler_params=pltpu.CompilerParams(
            dimension_semantics=("parallel", "arbitrary"), vmem_limit_bytes=VMEM_LIMIT),
        name="proj_in",
    )(x, w)


def _recur_consts():
    c = CHUNK
    w = GROUP_W
    r = np.arange(c)[:, None]
    lane = np.arange(w)[None, :]
    j = lane % c
    bd = (np.arange(w)[:, None] // c == lane // c)
    return dict(
        lincl=(np.arange(c)[None, :] <= r).astype(np.float32),
        ucat=(r <= j).astype(np.float32),
        incl=(j <= r).astype(np.float32),
        strict=(j < r).astype(np.float32),
        eye=(j == r).astype(np.float32),
        bd=bd.astype(np.float32),
    )


def _recur_kernel(l_valid, u0_ref, u1_ref, u2_ref, ubb_ref, hista_ref, histb_ref, sg0_ref, sh0_ref,
                  wa_ref, wb_ref, prm_ref, lincl_ref, ucat_ref, incl_ref, strict_ref, eye_ref, bd_ref,
                  y_ref, sg_out_ref, sh_out_ref, hista_out_ref,
                  sg_ref, sh_ref, xpa_ref, xpb_ref, hb_ref, hk_ref, hv_ref, hq_ref, ho_ref):
    c = CHUNK
    ci = pl.program_id(1)
    nc = pl.num_programs(1)

    @pl.when(ci == 0)
    def _():
        sg_ref[...] = sg0_ref[0]
        sh_ref[...] = sh0_ref[0]
        xpa_ref[0:HIST, :] = hista_ref[0]
        xpb_ref[0:HIST, :] = histb_ref[0]

    bd = bd_ref[...]
    bd_bf = bd.astype(BF16)
    lincl_bf = lincl_ref[...].astype(BF16)
    ones_bf = jnp.ones((c, c), BF16)
    incl = incl_ref[...] > 0.5
    strict = strict_ref[...] > 0.5
    eye = eye_ref[...]

    def blockdiag(x):
        return jnp.concatenate([x, x, x, x], axis=0) * bd

    if l_valid < c:
        valid = lax.broadcasted_iota(jnp.int32, (c, GROUP_W), 0) < l_valid
    else:
        valid = None

    def mask_rows(x):
        return x if valid is None else jnp.where(valid, x, 0.0)

    a_in = u0_ref[:, C_AIN:C_AIN + 256]
    a_gb = u0_ref[:, C_AGB:C_AGB + 256]
    a_gc = u0_ref[:, C_AGC:C_AGC + 256]
    b_z = u0_ref[:, C_BZ:C_BZ + 256]
    xpa_ref[HIST:HIST + c, :] = a_gc * a_in
    conv_a = jnp.zeros((c, GROUP_W), F32)
    for jj in range(CONV_A_W):
        off = HIST - (CONV_A_W - 1) + jj
        conv_a = conv_a + xpa_ref[off:off + c, :] * wa_ref[jj:jj + 1, :]
    y_ref[:, 0:256] = a_gb * conv_a
    hista_out_ref[0] = xpa_ref[l_valid:l_valid + HIST, :]
    xpa_ref[0:HIST, :] = xpa_ref[c:c + HIST, :]

    xpb_ref[HIST:HIST + c, :] = u1_ref[:, 0:768]
    conv_b = jnp.zeros((c, 768), F32)
    for jj in range(CONV_B_W):
        off = HIST - (CONV_B_W - 1) + jj
        conv_b = conv_b + xpb_ref[off:off + c, :] * wb_ref[jj:jj + 1, :]
    xpb_ref[0:HIST, :] = xpb_ref[c:c + HIST, :]
    conv_b = _silu(conv_b)
    q_raw = conv_b[:, 0:256]
    k_raw = conv_b[:, 256:512]
    v = conv_b[:, 512:768]
    q = q_raw * lax.rsqrt(_segsum(q_raw * q_raw, bd_bf) + RMS_EPS) * (HEAD_DIM ** -0.5)
    k = k_raw * lax.rsqrt(_segsum(k_raw * k_raw, bd_bf) + RMS_EPS)
    beta = mask_rows(jax.nn.sigmoid(ubb_ref[...]))
    neg_a = prm_ref[0:1, :]
    dt_bias = prm_ref[1:2, :]
    g = mask_rows(neg_a * _softplus(u1_ref[:, 768:1024] + dt_bias))

    gcol = _dot_c3(lincl_bf, g)
    grow = _dot_c3(ones_bf, g * ucat_ref[...])
    decay = jnp.exp(jnp.where(incl, gcol - grow, NEG))
    k_bf = k.astype(BF16)
    kb_bf = blockdiag(k).astype(BF16)
    kk = _dot_nt(k_bf, kb_bf)
    qk = _dot_nt(q.astype(BF16), kb_bf)
    a_mat = jnp.where(strict, kk * decay * beta, 0.0)
    p_mat = qk * decay

    def mmc(x, y):
        yb = blockdiag(y)
        xh, xl = _split2(x)
        yh, yl = _split2(yb)
        return _dot(xh, yh) + _dot(xl, yh) + _dot(xh, yl)

    t_mat = eye - a_mat
    pw = a_mat
    for _ in range(5):
        pw = mmc(pw, pw)
        t_mat = t_mat + mmc(pw, t_mat)

    egc = jnp.exp(gcol)
    rhs = jnp.concatenate([blockdiag(v * beta), blockdiag(k * beta * egc)], axis=1)
    th, tl = _split2(t_mat)
    rh, rl = _split2(rhs)
    sol = _dot(th, rh) + _dot(tl, rh) + _dot(th, rl)
    u_sol = sol[:, 0:256]
    w_sol = sol[:, 256:512]

    sg = sg_ref[...]
    sg_bf = sg.astype(BF16)
    v_new = u_sol - _dot(w_sol.astype(BF16), sg_bf)
    o_b = _dot((q * egc).astype(BF16), sg_bf) + _dot(p_mat.astype(BF16), blockdiag(v_new).astype(BF16))
    glast = gcol[c - 1:c, :]
    kd = k * jnp.exp(glast - gcol)
    sg_new = sg * jnp.exp(glast) + _dot_tn(kd.astype(BF16), v_new.astype(BF16)) * bd
    sg_ref[...] = sg_new
    ms_b = _segsum(o_b * o_b, bd_bf) * (1.0 / HEAD_DIM)
    y_ref[:, 256:512] = o_b * lax.rsqrt(ms_b + RMS_EPS) * prm_ref[2:3, :] * _silu(b_z)

    c_q = u2_ref[:, 0:256]
    c_f = u2_ref[:, 256:512]
    c_i = u2_ref[:, 512:768]
    c_g = u2_ref[:, 768:1024]
    lb = prm_ref[4:5, :]
    sig = jax.nn.sigmoid(c_f)
    log_f = mask_rows(jnp.log(lb + (1.0 - lb) * sig))
    k_c = mask_rows((1.0 - lb) * (1.0 - sig))
    q_c = _silu(c_q)
    bcum = _dot_c3(lincl_bf, log_f)
    sh = sh_ref[...]
    hb_ref[...] = bcum
    hk_ref[...] = k_c
    hv_ref[...] = c_i
    hq_ref[...] = q_c
    ho_ref[...] = _dot_nt((q_c * jnp.exp(bcum)).astype(BF16), sh.astype(BF16))
    row = lax.broadcasted_iota(jnp.int32, (c, GROUP_W), 0)

    def body(j, carry):
        bj = hb_ref[pl.ds(j, 1), :]
        kj = hk_ref[pl.ds(j, 1), :]
        vj = hv_ref[pl.ds(j, 1), :]
        e = jnp.exp(jnp.where(row >= j, hb_ref[...] - bj, NEG))
        f = hq_ref[...] * e * kj
        att = _dot(f.astype(BF16), bd_bf)
        ho_ref[...] += att * vj
        return carry

    lax.fori_loop(0, c, body, 0)
    blast = bcum[c - 1:c, :]
    kd_c = k_c * jnp.exp(blast - bcum)
    sh_ref[...] = sh * jnp.exp(blast) + _dot_tn(c_i.astype(BF16), kd_c.astype(BF16)) * bd
    o_c = ho_ref[...]
    ms_c = _segsum(o_c * o_c, bd_bf) * (1.0 / HEAD_DIM)
    y_ref[:, 512:768] = o_c * lax.rsqrt(ms_c + RMS_EPS) * prm_ref[3:4, :] * _silu(c_g)

    @pl.when(ci == nc - 1)
    def _():
        sg_out_ref[0] = sg_ref[...]
        sh_out_ref[0] = sh_ref[...]


def _recur(u, l_valid, hist_a, hist_b, sg0, sh0, wa, wb, prm):
    bsz = sg0.shape[0]
    n = u.shape[0]
    nc = n // bsz // CHUNK
    consts = _recur_consts()
    cnames = ("lincl", "ucat", "incl", "strict", "eye", "bd")
    cvals = [jnp.asarray(consts[k]) for k in cnames]
    row_map = lambda col: (lambda b, ci: (b * nc + ci, col))
    full2 = lambda b, ci: (0, 0)
    per_b = lambda b, ci: (b, 0, 0)
    in_specs = [
        pl.BlockSpec((CHUNK, 1024), row_map(0)),
        pl.BlockSpec((CHUNK, 1024), row_map(1)),
        pl.BlockSpec((CHUNK, 1024), row_map(2)),
        pl.BlockSpec((CHUNK, 256), row_map(C_BB // 256)),
        pl.BlockSpec((1, HIST, 256), per_b),
        pl.BlockSpec((1, HIST, 768), per_b),
        pl.BlockSpec((1, 256, 256), per_b),
        pl.BlockSpec((1, 256, 256), per_b),
        pl.BlockSpec(wa.shape, full2),
        pl.BlockSpec(wb.shape, full2),
        pl.BlockSpec(prm.shape, full2),
    ] + [pl.BlockSpec(cv.shape, full2) for cv in cvals]
    out_shape = (jax.ShapeDtypeStruct((n, 768), F32),
                 jax.ShapeDtypeStruct((bsz, 256, 256), F32),
                 jax.ShapeDtypeStruct((bsz, 256, 256), F32),
                 jax.ShapeDtypeStruct((bsz, HIST, 256), F32))
    out_specs = (pl.BlockSpec((CHUNK, 768), lambda b, ci: (b * nc + ci, 0)),
                 pl.BlockSpec((1, 256, 256), per_b),
                 pl.BlockSpec((1, 256, 256), per_b),
                 pl.BlockSpec((1, HIST, 256), per_b))
    scratch = [pltpu.VMEM((256, 256), F32), pltpu.VMEM((256, 256), F32),
               pltpu.VMEM((CHUNK + HIST, 256), F32), pltpu.VMEM((CHUNK + HIST, 768), F32),
               pltpu.VMEM((CHUNK, 256), F32), pltpu.VMEM((CHUNK, 256), F32),
               pltpu.VMEM((CHUNK, 256), F32), pltpu.VMEM((CHUNK, 256), F32),
               pltpu.VMEM((CHUNK, 256), F32)]
    return pl.pallas_call(
        functools.partial(_recur_kernel, l_valid),
        out_shape=out_shape,
        grid=(bsz, nc),
        in_specs=in_specs,
        out_specs=out_specs,
        scratch_shapes=scratch,
        compiler_params=pltpu.CompilerParams(
            dimension_semantics=("parallel", "arbitrary"), vmem_limit_bytes=VMEM_LIMIT),
        name="recur",
    )(u, u, u, u, hist_a, hist_b, sg0, sh0, wa, wb, prm, *cvals)


def _t5_bucket_np(dist):
    n = np.maximum(dist, 0)
    nf = np.maximum(n, MAX_EXACT).astype(np.float32)
    ratio = (np.log(nf / np.float32(MAX_EXACT)) / np.float32(math.log(MAX_DISTANCE / MAX_EXACT))).astype(np.float32)
    large = MAX_EXACT + (ratio * np.float32(N_BUCKETS - MAX_EXACT)).astype(np.int32)
    large = np.minimum(large, N_BUCKETS - 1)
    return np.where(n < MAX_EXACT, n, large).astype(np.int32)


def _bias_from_buckets(bkt, relb_ref, h):
    out = jnp.full(bkt.shape, NEG, F32)
    for b in range(N_BUCKETS):
        out = jnp.where(bkt == b, relb_ref[b, h], out)
    return out


def _head_lane_mask(h, shape):
    lane = lax.broadcasted_iota(jnp.int32, shape, len(shape) - 1)
    return (lane >= h * HEAD_DIM) & (lane < (h + 1) * HEAD_DIM)


def _map_lane_mask(h, m, shape):
    lane = lax.broadcasted_iota(jnp.int32, shape, len(shape) - 1)
    lo = h * HEAD_DIM + m * HEAD_DIM_DQK
    return (lane >= lo) & (lane < lo + HEAD_DIM_DQK)


def _attn_finish(o1, o2, lam_ref, nd_ref, bd_bf):
    od = o1 - lam_ref[...] * o2
    ms = _segsum(od * od, bd_bf) * (1.0 / HEAD_DIM)
    return od * lax.rsqrt(ms + RMS_EPS) * nd_ref[...]


def _attn_prompt_kernel(relb_ref, q_ref, k_ref, v_ref, bkt_ref, lam_ref, nd_ref, bd_ref, o_ref,
                        q8_ref, m_ref, l_ref, acc_ref, bias_ref):
    b = pl.program_id(0)
    i = pl.program_id(1)
    j = pl.program_id(2)
    tq = q_ref.shape[0]

    @pl.when((b == 0) & (i == 0) & (j == 0))
    def _():
        for d in range(3):
            bkt = bkt_ref[d]
            for h in range(N_HEADS):
                bias_ref[d, h] = _bias_from_buckets(bkt, relb_ref, h)

    @pl.when(j == 0)
    def _():
        qs = q_ref[...] * (HEAD_DIM_DQK ** -0.5)
        for h in range(N_HEADS):
            for m in range(2):
                q8_ref[2 * h + m] = jnp.where(_map_lane_mask(h, m, qs.shape), qs, 0.0).astype(BF16)
        m_ref[...] = jnp.full(m_ref.shape, NEG, F32)
        l_ref[...] = jnp.zeros(l_ref.shape, F32)
        acc_ref[...] = jnp.zeros(acc_ref.shape, F32)

    @pl.when(j <= i)
    def _():
        kb = k_ref[...].astype(BF16)
        vb = v_ref[...].astype(BF16)
        d = jnp.minimum(i - j, 2)
        for hm in range(2 * N_HEADS):
            s = _dot_nt(q8_ref[hm], kb) + bias_ref[d, hm // 2]
            m_old = m_ref[hm]
            m_new = jnp.maximum(m_old, jnp.max(s, axis=1, keepdims=True))
            p = jnp.exp(s - m_new)
            alpha = jnp.exp(m_old - m_new)
            l_ref[hm] = alpha * l_ref[hm] + jnp.sum(p, axis=1, keepdims=True)
            acc_ref[hm] = alpha * acc_ref[hm] + _dot(p.astype(BF16), vb)
            m_ref[hm] = m_new

    @pl.when(j == i)
    def _():
        o1 = jnp.zeros((tq, GROUP_W), F32)
        o2 = jnp.zeros((tq, GROUP_W), F32)
        for h in range(N_HEADS):
            hm = _head_lane_mask(h, (tq, GROUP_W))
            o1 = jnp.where(hm, acc_ref[2 * h] / l_ref[2 * h], o1)
            o2 = jnp.where(hm, acc_ref[2 * h + 1] / l_ref[2 * h + 1], o2)
        o_ref[...] = _attn_finish(o1, o2, lam_ref, nd_ref, bd_ref[...].astype(BF16))


def _attn_prompt(u, bsz, seq, rel_bias, lam_row, nd_row, bd):
    t = min(256, seq)
    nq = seq // t
    bkt = np.stack([
        np.where(d * t + np.arange(t)[:, None] - np.arange(t)[None, :] >= 0,
                 _t5_bucket_np(d * t + np.arange(t)[:, None] - np.arange(t)[None, :]), -1)
        for d in range(3)]).astype(np.int32)
    assert nq <= 2 or t >= MAX_DISTANCE, "blocks two or more tiles back must share the last bucket"
    full2 = lambda b, i, j: (0, 0)
    return pl.pallas_call(
        _attn_prompt_kernel,
        out_shape=jax.ShapeDtypeStruct((bsz * seq, GROUP_W), F32),
        grid=(bsz, nq, nq),
        in_specs=[
            pl.BlockSpec(memory_space=pltpu.SMEM),
            pl.BlockSpec((t, GROUP_W), lambda b, i, j: (b * nq + i, C_DQ // 256)),
            pl.BlockSpec((t, GROUP_W), lambda b, i, j: (b * nq + jnp.minimum(j, i), C_DK // 256)),
            pl.BlockSpec((t, GROUP_W), lambda b, i, j: (b * nq + jnp.minimum(j, i), C_DV // 256)),
            pl.BlockSpec((3, t, t), lambda b, i, j: (0, 0, 0)),
            pl.BlockSpec((1, GROUP_W), full2),
            pl.BlockSpec((1, GROUP_W), full2),
            pl.BlockSpec((GROUP_W, GROUP_W), full2),
        ],
        out_specs=pl.BlockSpec((t, GROUP_W), lambda b, i, j: (b * nq + i, 0)),
        scratch_shapes=[pltpu.VMEM((8, t, GROUP_W), BF16), pltpu.VMEM((8, t, 1), F32),
                        pltpu.VMEM((8, t, 1), F32), pltpu.VMEM((8, t, GROUP_W), F32),
                        pltpu.VMEM((3, N_HEADS, t, t), F32)],
        compiler_params=pltpu.CompilerParams(
            dimension_semantics=("arbitrary", "arbitrary", "arbitrary"), vmem_limit_bytes=VMEM_LIMIT),
        name="attn_prompt",
    )(rel_bias, u, u, u, jnp.asarray(bkt), lam_row, nd_row, bd)


def _attn_sample_kernel(pp, n_steps, pt_ref, relb_ref, q_ref, kn_ref, vn_ref, *rest):
    k_refs = rest[:pp]
    v_refs = rest[pp:2 * pp]
    (bktl_ref, bktn_ref, lam_ref, nd_ref, bd_ref, o_ref,
     q64_ref, m_ref, l_ref, acc_ref, bias_ref, biasn_ref) = rest[2 * pp:]
    b = pl.program_id(0)
    s_id = pl.program_id(1)
    rows = 8
    page = k_refs[0].shape[1]
    width = pp * page

    @pl.when((b == 0) & (s_id == 0))
    def _():
        for h in range(N_HEADS):
            near = _bias_from_buckets(bktl_ref[...], relb_ref, h)
            far = jnp.full((rows, page), relb_ref[N_BUCKETS - 1, h], F32)
            newb = _bias_from_buckets(bktn_ref[...], relb_ref, h)
            for m in range(2):
                r0 = (2 * h + m) * rows
                for t in range(pp):
                    bias_ref[0, r0:r0 + rows, t * page:(t + 1) * page] = far
                    bias_ref[1, r0:r0 + rows, t * page:(t + 1) * page] = near if t == pp - 1 else far
                biasn_ref[r0:r0 + rows, :] = newb

    @pl.when(s_id == 0)
    def _():
        qs = q_ref[0] * (HEAD_DIM_DQK ** -0.5)
        for h in range(N_HEADS):
            for m in range(2):
                r0 = (2 * h + m) * rows
                q64_ref[r0:r0 + rows, :] = jnp.where(_map_lane_mask(h, m, qs.shape), qs, 0.0)
        s = _dot_nt(q64_ref[...].astype(BF16), kn_ref[0].astype(BF16)) + biasn_ref[...]
        m0 = jnp.max(s, axis=1, keepdims=True)
        p = jnp.exp(s - m0)
        m_ref[...] = m0
        l_ref[...] = jnp.sum(p, axis=1, keepdims=True)
        acc_ref[...] = _dot(p.astype(BF16), vn_ref[0].astype(BF16))

    kb = jnp.concatenate([r[0] for r in k_refs], axis=0).astype(BF16)
    vb = jnp.concatenate([r[0] for r in v_refs], axis=0).astype(BF16)
    sel = jnp.where(s_id == n_steps - 1, 1, 0)
    s = _dot_nt(q64_ref[...].astype(BF16), kb) + bias_ref[sel]
    m_old = m_ref[...]
    m_new = jnp.maximum(m_old, jnp.max(s, axis=1, keepdims=True))
    p = jnp.exp(s - m_new)
    alpha = jnp.exp(m_old - m_new)
    l_ref[...] = alpha * l_ref[...] + jnp.sum(p, axis=1, keepdims=True)
    acc_ref[...] = alpha * acc_ref[...] + _dot(p.astype(BF16), vb)
    m_ref[...] = m_new

    @pl.when(s_id == n_steps - 1)
    def _():
        o1 = jnp.zeros((rows, GROUP_W), F32)
        o2 = jnp.zeros((rows, GROUP_W), F32)
        for h in range(N_HEADS):
            hm = _head_lane_mask(h, (rows, GROUP_W))
            r1 = (2 * h) * rows
            r2 = (2 * h + 1) * rows
            o1 = jnp.where(hm, acc_ref[r1:r1 + rows, :] / l_ref[r1:r1 + rows, :], o1)
            o2 = jnp.where(hm, acc_ref[r2:r2 + rows, :] / l_ref[r2:r2 + rows, :], o2)
        o_ref[0] = _attn_finish(o1, o2, lam_ref, nd_ref, bd_ref[...].astype(BF16))


def _attn_sample(q8, k8, v8, ck, cv, page_table, l_new, rel_bias, lam_row, nd_row, bd):
    bsz = q8.shape[0]
    n_pages = page_table.shape[1]
    page = ck.shape[1]
    pp = 8
    while n_pages % pp:
        pp //= 2
    n_steps = n_pages // pp
    past = n_pages * page
    qpos = past + np.arange(8)[:, None]
    bkt_last = _t5_bucket_np(qpos - (past - page + np.arange(page))[None, :])
    dist_new = np.arange(8)[:, None] - np.arange(8)[None, :]
    ok = (dist_new >= 0) & (np.arange(8)[None, :] < l_new)
    bkt_new = np.where(ok, _t5_bucket_np(dist_new), -1).astype(np.int32)
    assert page >= MAX_DISTANCE, "only the last cache page may need distance-dependent bias"

    def page_map(t):
        return lambda b, s, pt: (pt[b, s * pp + t], 0, 0)

    per_b = lambda b, s, pt: (b, 0, 0)
    full2 = lambda b, s, pt: (0, 0)
    in_specs = ([pl.BlockSpec(memory_space=pltpu.SMEM),
                 pl.BlockSpec((1, 8, GROUP_W), per_b),
                 pl.BlockSpec((1, 8, GROUP_W), per_b),
                 pl.BlockSpec((1, 8, GROUP_W), per_b)]
                + [pl.BlockSpec((1, page, GROUP_W), page_map(t)) for t in range(pp)]
                + [pl.BlockSpec((1, page, GROUP_W), page_map(t)) for t in range(pp)]
                + [pl.BlockSpec((8, page), full2), pl.BlockSpec((8, 8), full2),
                   pl.BlockSpec((1, GROUP_W), full2), pl.BlockSpec((1, GROUP_W), full2),
                   pl.BlockSpec((GROUP_W, GROUP_W), full2)])
    grid_spec = pltpu.PrefetchScalarGridSpec(
        num_scalar_prefetch=1,
        grid=(bsz, n_steps),
        in_specs=in_specs,
        out_specs=pl.BlockSpec((1, 8, GROUP_W), per_b),
        scratch_shapes=[pltpu.VMEM((64, GROUP_W), F32), pltpu.VMEM((64, 1), F32),
                        pltpu.VMEM((64, 1), F32), pltpu.VMEM((64, GROUP_W), F32),
                        pltpu.VMEM((2, 64, pp * page), F32), pltpu.VMEM((64, 8), F32)],
    )
    return pl.pallas_call(
        functools.partial(_attn_sample_kernel, pp, n_steps),
        out_shape=jax.ShapeDtypeStruct((bsz, 8, GROUP_W), F32),
        grid_spec=grid_spec,
        compiler_params=pltpu.CompilerParams(
            dimension_semantics=("arbitrary", "arbitrary"), vmem_limit_bytes=VMEM_LIMIT),
        name="attn_sample",
    )(page_table, rel_bias, q8, k8, v8, *([ck] * pp), *([cv] * pp),
      jnp.asarray(bkt_last), jnp.asarray(bkt_new), lam_row, nd_row, bd)


def _oproj_kernel(alpha, yabc_ref, yd_ref, x_ref, wo_ref, g_ref, b_ref, o_ref):
    mix = (_dot(yabc_ref[...].astype(BF16), wo_ref[0:768, :])
           + _dot(yd_ref[...].astype(BF16), wo_ref[768:1024, :]))
    o_ref[...] = _layer_norm(alpha * x_ref[...] + mix, g_ref[...], b_ref[...])


def _oproj(yabc, yd, x, wo, g, b, alpha, tm):
    n = x.shape[0]
    row = lambda i: (i, 0)
    full = lambda i: (0, 0)
    return pl.pallas_call(
        functools.partial(_oproj_kernel, alpha),
        out_shape=jax.ShapeDtypeStruct((n, D_MODEL), F32),
        grid=(n // tm,),
        in_specs=[pl.BlockSpec((tm, 768), row), pl.BlockSpec((tm, 256), row),
                  pl.BlockSpec((tm, D_MODEL), row), pl.BlockSpec((D_MODEL, D_MODEL), full),
                  pl.BlockSpec((1, D_MODEL), full), pl.BlockSpec((1, D_MODEL), full)],
        out_specs=pl.BlockSpec((tm, D_MODEL), row),
        compiler_params=pltpu.CompilerParams(
            dimension_semantics=("parallel",), vmem_limit_bytes=VMEM_LIMIT),
        name="out_proj_ln",
    )(yabc, yd, x, wo, g, b)


def _ffn_kernel(alpha, moe, x_ref, wr_ref, wg_ref, wu_ref, wd_ref, g_ref, b_ref, o_ref,
                xb_ref, acc_ref, comb_ref):
    e = pl.program_id(1)
    f = pl.program_id(2)
    last = (e == pl.num_programs(1) - 1) & (f == pl.num_programs(2) - 1)

    @pl.when((e == 0) & (f == 0))
    def _():
        x = x_ref[...]
        xb_ref[...] = x.astype(BF16)
        acc_ref[...] = jnp.zeros(acc_ref.shape, F32)
        if moe:
            logits = jnp.dot(x, wr_ref[...], preferred_element_type=F32, precision=lax.Precision.HIGHEST)
            lane = lax.broadcasted_iota(jnp.int32, logits.shape, 1)
            logits = jnp.where(lane < N_EXPERTS, logits, NEG)
            big = logits.shape[1]
            m1 = jnp.max(logits, axis=1, keepdims=True)
            i1 = jnp.min(jnp.where(logits == m1, lane, big), axis=1, keepdims=True)
            rest = jnp.where(lane == i1, NEG, logits)
            m2 = jnp.max(rest, axis=1, keepdims=True)
            i2 = jnp.min(jnp.where(rest == m2, lane, big), axis=1, keepdims=True)
            e2 = jnp.exp(m2 - m1)
            g1 = 1.0 / (1.0 + e2)
            g2 = e2 / (1.0 + e2)
            comb_ref[...] = jnp.where(lane == i1, g1, 0.0) + jnp.where(lane == i2, g2, 0.0)

    xb = xb_ref[...]
    h = _silu(_dot(xb, wg_ref[0])) * _dot(xb, wu_ref[0])
    if moe:
        lane = lax.broadcasted_iota(jnp.int32, comb_ref.shape, 1)
        h = h * jnp.sum(jnp.where(lane == e, comb_ref[...], 0.0), axis=1, keepdims=True)
    acc_ref[...] += _dot(h.astype(BF16), wd_ref[0])

    @pl.when(last)
    def _():
        o_ref[...] = _layer_norm(alpha * x_ref[...] + acc_ref[...], g_ref[...], b_ref[...])


def _ffn(x, wr, wg, wu, wd, g, b, alpha, moe, tm, tf):
    n = x.shape[0]
    n_e, _, ff = wg.shape
    row = lambda i, e, f: (i, 0)
    full = lambda i, e, f: (0, 0)
    return pl.pallas_call(
        functools.partial(_ffn_kernel, alpha, moe),
        out_shape=jax.ShapeDtypeStruct((n, D_MODEL), F32),
        grid=(n // tm, n_e, ff // tf),
        in_specs=[pl.BlockSpec((tm, D_MODEL), row),
                  pl.BlockSpec(wr.shape, full),
                  pl.BlockSpec((1, D_MODEL, tf), lambda i, e, f: (e, 0, f)),
                  pl.BlockSpec((1, D_MODEL, tf), lambda i, e, f: (e, 0, f)),
                  pl.BlockSpec((1, tf, D_MODEL), lambda i, e, f: (e, f, 0)),
                  pl.BlockSpec((1, D_MODEL), full), pl.BlockSpec((1, D_MODEL), full)],
        out_specs=pl.BlockSpec((tm, D_MODEL), row),
        scratch_shapes=[pltpu.VMEM((tm, D_MODEL), BF16), pltpu.VMEM((tm, D_MODEL), F32),
                        pltpu.VMEM((tm, 128), F32)],
        compiler_params=pltpu.CompilerParams(
            dimension_semantics=("parallel", "arbitrary", "arbitrary"), vmem_limit_bytes=VMEM_LIMIT),
        name="moe_ln" if moe else "ffn_ln",
    )(x, wr, wg, wu, wd, g, b)


def _w_in_columns():
    o = {}
    off = 0
    for name, size in (("a_in", 256), ("a_gb", 256), ("a_gc", 256), ("b_qkv", 768), ("b_a", 4), ("b_b", 4),
                       ("b_z", 256), ("c_q", 256), ("c_f", 256), ("c_i", 256), ("c_g", 256),
                       ("d_q", 256), ("d_k", 256), ("d_v", 256)):
        o[name] = np.arange(off, off + size)
        off += size
    rep = lambda ix: np.repeat(ix, HEAD_DIM)
    cols = np.concatenate([o["a_in"], o["a_gb"], o["a_gc"], o["b_z"],
                           o["b_qkv"], rep(o["b_a"]),
                           o["c_q"], o["c_f"], o["c_i"], o["c_g"],
                           rep(o["b_b"]), o["d_q"], o["d_k"], o["d_v"]])
    assert cols.shape[0] == U_W
    return cols


def _pick_tile(n, pref):
    t = min(pref, n)
    while n % t:
        t //= 2
    return t


def _blockdiag_state(s):
    bsz = s.shape[0]
    eye = jnp.eye(N_HEADS, dtype=s.dtype)
    return (s[:, :, :, None, :] * eye[None, :, None, :, None]).reshape(bsz, GROUP_W, GROUP_W)


def _diag_blocks(s):
    bsz = s.shape[0]
    s5 = s.reshape(bsz, N_HEADS, HEAD_DIM, N_HEADS, HEAD_DIM)
    return jnp.stack([s5[:, h, :, h, :] for h in range(N_HEADS)], axis=1)


def _run_group(x, init, attend, lw, depth):
    bsz, seq, _ = x.shape
    lp = -(-seq // CHUNK) * CHUNK
    assert (lp == seq or seq < CHUNK) and seq >= CONV_B_W - 1
    n = bsz * seq
    alpha = (2 * depth) ** 0.25
    xf = x.reshape(n, D_MODEL)
    tm = _pick_tile(n, 512)
    outs = []
    for l in range(depth):
        w = lw[l]
        u = _proj(xf, w["w_in"], tm, 1024)
        u3 = u.reshape(bsz, seq, U_W)
        if init is None:
            hist_a = jnp.zeros((bsz, HIST, 256), F32)
            hist_b = jnp.zeros((bsz, HIST, 768), F32)
            sg0 = jnp.zeros((bsz, GROUP_W, GROUP_W), F32)
            sh0 = sg0
        else:
            ca, cb, s_g, s_h = init[l]
            hist_a = jnp.pad(ca, ((0, 0), (HIST - ca.shape[1], 0), (0, 0)))
            hist_b = jnp.pad(cb, ((0, 0), (HIST - cb.shape[1], 0), (0, 0)))
            sg0 = _blockdiag_state(s_g)
            sh0 = _blockdiag_state(jnp.swapaxes(s_h, 2, 3))
        u_rec = u if lp == seq else jnp.pad(u3, ((0, 0), (0, lp - seq), (0, 0))).reshape(bsz * lp, U_W)
        yabc, sg1, sh1, tail_a = _recur(u_rec, min(seq, CHUNK) if lp != seq else CHUNK,
                                        hist_a, hist_b, sg0, sh0, w["conv_a"], w["conv_b"], w["prm"])
        if lp != seq:
            yabc = yabc.reshape(bsz, lp, 768)[:, :seq].reshape(n, 768)
        yd = attend(l, u, u3, w)
        x1 = _oproj(yabc, yd, xf, w["w_o"], w["ln1_g"], w["ln1_b"], alpha, tm)
        f = w["ffn"]
        xf = _ffn(x1, f["wr"], f["wg"], f["wu"], f["wd"], w["ln2_g"], w["ln2_b"], alpha, f["moe"],
                  _pick_tile(n, 512), f["tf"])
        a0 = HIST - (CONV_A_W - 1)
        outs.append((
            u3[:, :, C_DK:C_DK + 256].reshape(bsz, seq, N_HEADS, HEAD_DIM),
            u3[:, :, C_DV:C_DV + 256].reshape(bsz, seq, N_HEADS, HEAD_DIM),
            tail_a[:, a0:a0 + CONV_A_W - 1],
            u3[:, seq - (CONV_B_W - 1):, C_BQKV:C_BQKV + 768],
            _diag_blocks(sg1),
            jnp.swapaxes(_diag_blocks(sh1), 2, 3),
        ))
    return xf.reshape(bsz, seq, D_MODEL), [jnp.stack([o[i] for o in outs]) for i in range(6)]


def kernel(x_prompt, x_sample, cache_k, cache_v, state_conv_a, state_conv_b, state_gdn, state_hgrn, page_table, w_in, conv_a, conv_b, gdn_a_log, gdn_dt_bias, norm_b, lower_bounds, norm_c, lambda_q1, lambda_k1, lambda_q2, lambda_k2, norm_d, rel_bias, w_o, ln1_g, ln1_b, ffn_w_gate, ffn_w_up, ffn_w_down, router_w, moe_w_gate, moe_w_up, moe_w_down, ln2_g, ln2_b):
    depth = w_in.shape[0]
    cols = _w_in_columns()
    rep = lambda t: jnp.repeat(t.astype(F32), HEAD_DIM)
    tile4 = lambda t: jnp.tile(t.astype(F32), N_HEADS)
    lbs = jax.nn.softmax(lower_bounds.astype(F32), axis=0)
    lb_all = jnp.cumsum(lbs, axis=0) - lbs[0]
    bd = jnp.asarray(_recur_consts()["bd"])
    lw = []
    for l in range(depth):
        lam_init = 0.8 - 0.6 * math.exp(-0.3 * l)
        lam = (jnp.exp(jnp.sum(lambda_q1[l].astype(F32) * lambda_k1[l].astype(F32)))
               - jnp.exp(jnp.sum(lambda_q2[l].astype(F32) * lambda_k2[l].astype(F32))) + lam_init)
        prm = jnp.stack([-jnp.exp(rep(gdn_a_log[l])), rep(gdn_dt_bias[l]), tile4(norm_b[l]),
                         tile4(norm_c[l]), lb_all[l], jnp.zeros((GROUP_W,), F32),
                         jnp.zeros((GROUP_W,), F32), jnp.zeros((GROUP_W,), F32)])
        j = l // 2
        if l % 2 == 0:
            ff = ffn_w_gate.shape[2]
            ffn = dict(moe=False, wr=jnp.zeros((D_MODEL, 128), F32),
                       wg=ffn_w_gate[j][None].astype(BF16), wu=ffn_w_up[j][None].astype(BF16),
                       wd=ffn_w_down[j][None].astype(BF16))
        else:
            ff = moe_w_gate.shape[3]
            ffn = dict(moe=True, wr=jnp.pad(router_w[j].astype(F32), ((0, 0), (0, 128 - N_EXPERTS))),
                       wg=moe_w_gate[j].astype(BF16), wu=moe_w_up[j].astype(BF16),
                       wd=moe_w_down[j].astype(BF16))
        tf = ff
        for cand in (1408, 1024, 512, 256, 128):
            if ff % cand == 0:
                tf = cand
                break
        ffn["tf"] = tf
        lw.append(dict(
            w_in=w_in[l][:, cols].astype(BF16),
            conv_a=jnp.pad(conv_a[l].astype(F32), ((0, HIST - CONV_A_W), (0, 0))),
            conv_b=jnp.pad(conv_b[l].astype(F32), ((0, HIST - CONV_B_W), (0, 0))),
            prm=prm,
            lam_row=jnp.full((1, GROUP_W), lam, F32),
            nd_row=(tile4(norm_d[l]) * (1.0 - lam_init))[None, :],
            w_o=w_o[l].astype(BF16),
            ln1_g=ln1_g[l][None].astype(F32), ln1_b=ln1_b[l][None].astype(F32),
            ln2_g=ln2_g[l][None].astype(F32), ln2_b=ln2_b[l][None].astype(F32),
            ffn=ffn,
        ))
    relb = rel_bias.astype(F32)

    def attend_prompt(l, u, u3, w):
        bsz, seq, _ = u3.shape
        return _attn_prompt(u, bsz, seq, relb, w["lam_row"], w["nd_row"], bd)

    n_pool, page = cache_k.shape[1], cache_k.shape[2]

    def attend_sample(l, u, u3, w):
        bsz, seq, _ = u3.shape
        pad8 = lambda t: jnp.pad(t, ((0, 0), (0, 8 - seq), (0, 0)))
        o = _attn_sample(pad8(u3[:, :, C_DQ:C_DQ + 256]), pad8(u3[:, :, C_DK:C_DK + 256]),
                         pad8(u3[:, :, C_DV:C_DV + 256]),
                         cache_k[l].reshape(n_pool, page, GROUP_W), cache_v[l].reshape(n_pool, page, GROUP_W),
                         page_table, seq, relb, w["lam_row"], w["nd_row"], bd)
        return o[:, :seq].reshape(bsz * seq, GROUP_W)

    init_s = [(state_conv_a[l], state_conv_b[l], state_gdn[l], state_hgrn[l]) for l in range(depth)]
    y_p, (k_p, v_p, ca_p, cb_p, sg_p, sh_p) = _run_group(x_prompt, None, attend_prompt, lw, depth)
    y_s, (k_s, v_s, ca_s, cb_s, sg_s, sh_s) = _run_group(x_sample, init_s, attend_sample, lw, depth)
    return (y_p, y_s, k_p, v_p, k_s, v_s, ca_p, ca_s, cb_p, cb_s, sg_p, sg_s, sh_p, sh_s)
```

```python
import functools
import math

import numpy as np
import jax
import jax.numpy as jnp
from jax import lax
from jax.experimental import pallas as pl
from jax.experimental.pallas import tpu as pltpu

F32 = jnp.float32
BF16 = jnp.bfloat16

D_MODEL = 1024
GROUP_W = 256
N_HEADS = 4
HEAD_DIM = 64
CHUNK = 64
HIST = 8
SUB = 16
SUB_SHIFT = 4
CONV_A_W = 3
CONV_B_W = 4
HEAD_DIM_DQK = 32
N_BUCKETS = 32
MAX_EXACT = 16
MAX_DISTANCE = 128
N_EXPERTS = 8
LN_EPS = 1e-5
RMS_EPS = 1e-6
NEG = -1e30
U_W = 4096
VMEM_LIMIT = 56 * 1024 * 1024

C_AIN, C_AGB, C_AGC, C_BZ = 0, 256, 512, 768
C_BQKV, C_BA = 1024, 1792
C_CQ, C_CF, C_CI, C_CG = 2048, 2304, 2560, 2816
C_BB, C_DQ, C_DK, C_DV = 3072, 3328, 3584, 3840


def _dot(a, b):
    return jnp.dot(a, b, preferred_element_type=F32)


def _dot_nt(a, b):
    return lax.dot_general(a, b, (((1,), (1,)), ((), ())), preferred_element_type=F32)


def _dot_tn(a, b):
    return lax.dot_general(a, b, (((0,), (0,)), ((), ())), preferred_element_type=F32)


def _split2(x):
    hi = x.astype(BF16)
    lo = (x - hi.astype(F32)).astype(BF16)
    return hi, lo


def _split3(x):
    hi = x.astype(BF16)
    r = x - hi.astype(F32)
    mid = r.astype(BF16)
    lo = (r - mid.astype(F32)).astype(BF16)
    return hi, mid, lo


def _dot_c3(c_bf, x):
    hi, mid, lo = _split3(x)
    return _dot(c_bf, hi) + _dot(c_bf, mid) + _dot(c_bf, lo)


def _segsum(s, bd_bf):
    hi, lo = _split2(s)
    return _dot(hi, bd_bf) + _dot(lo, bd_bf)


def _silu(x):
    return x * jax.nn.sigmoid(x)


def _softplus(x):
    return jnp.maximum(x, 0.0) + jnp.log1p(jnp.exp(-jnp.abs(x)))


def _layer_norm(z, g, b):
    mu = jnp.mean(z, axis=-1, keepdims=True)
    zc = z - mu
    var = jnp.mean(zc * zc, axis=-1, keepdims=True)
    return zc * lax.rsqrt(var + LN_EPS) * g + b


def _proj_kernel(tn, x_ref, w_ref, o_ref):
    xb = x_ref[...].astype(BF16)
    for c0 in range(0, w_ref.shape[1], tn):
        o_ref[:, c0:c0 + tn] = _dot(xb, w_ref[:, c0:c0 + tn])


def _proj(x, w, tm, tn):
    n, k = x.shape
    m = w.shape[1]
    return pl.pallas_call(
        functools.partial(_proj_kernel, tn),
        out_shape=jax.ShapeDtypeStruct((n, m), F32),
        grid=(n // tm,),
        in_specs=[pl.BlockSpec((tm, k), lambda i: (i, 0)),
                  pl.BlockSpec((k, m), lambda i: (0, 0))],
        out_specs=pl.BlockSpec((tm, m), lambda i: (i, 0)),
        compiler_params=pltpu.CompilerParams(
            dimension_semantics=("parallel",), vmem_limit_bytes=VMEM_LIMIT),
        name="proj_in",
    )(x, w)


def _recur_consts():
    c = CHUNK
    w = GROUP_W
    r = np.arange(c)[:, None]
    lane = np.arange(w)[None, :]
    j = lane % c
    bd = (np.arange(w)[:, None] // c == lane // c)
    return dict(
        lincl=(np.arange(c)[None, :] <= r).astype(np.float32),
        ucat=(r <= j).astype(np.float32),
        incl=(j <= r).astype(np.float32),
        strict=(j < r).astype(np.float32),
        eye=(j == r).astype(np.float32),
        bd=bd.astype(np.float32),
    )


def _recur_kernel(l_valid, u0_ref, u1_ref, u2_ref, ubb_ref, hista_ref, histb_ref, sg0_ref, sh0_ref,
                  wa_ref, wb_ref, prm_ref, lincl_ref, ucat_ref, incl_ref, strict_ref, eye_ref, bd_ref,
                  y_ref, sg_out_ref, sh_out_ref, hista_out_ref,
                  sg_ref, sh_ref, xpa_ref, xpb_ref, hbp_ref, hkp_ref, gs_ref):
    c = CHUNK
    ci = pl.program_id(1)
    nc = pl.num_programs(1)

    @pl.when(ci == 0)
    def _():
        sg_ref[...] = sg0_ref[0]
        sh_ref[...] = sh0_ref[0]
        xpa_ref[0:HIST, :] = hista_ref[0]
        xpb_ref[0:HIST, :] = histb_ref[0]
        hbp_ref[0:SUB, :] = jnp.zeros((SUB, GROUP_W), F32)
        hkp_ref[0:SUB, :] = jnp.zeros((SUB, GROUP_W), F32)

    bd = bd_ref[...]
    bd_bf = bd.astype(BF16)
    lincl_bf = lincl_ref[...].astype(BF16)
    ones_bf = jnp.ones((c, c), BF16)
    incl = incl_ref[...] > 0.5
    strict = strict_ref[...] > 0.5
    eye = eye_ref[...]

    def blockdiag(x):
        return jnp.concatenate([x, x, x, x], axis=0) * (bd_bf if x.dtype == BF16 else bd)

    if l_valid < c:
        valid = lax.broadcasted_iota(jnp.int32, (c, GROUP_W), 0) < l_valid
    else:
        valid = None

    def mask_rows(x):
        return x if valid is None else jnp.where(valid, x, 0.0)

    a_in = u0_ref[:, C_AIN:C_AIN + 256]
    a_gb = u0_ref[:, C_AGB:C_AGB + 256]
    a_gc = u0_ref[:, C_AGC:C_AGC + 256]
    b_z = u0_ref[:, C_BZ:C_BZ + 256]
    xpa_ref[HIST:HIST + c, :] = a_gc * a_in
    conv_a = jnp.zeros((c, GROUP_W), F32)
    for jj in range(CONV_A_W):
        off = HIST - (CONV_A_W - 1) + jj
        conv_a = conv_a + xpa_ref[off:off + c, :] * wa_ref[jj:jj + 1, :]
    y_ref[:, 0:256] = a_gb * conv_a
    hista_out_ref[0] = xpa_ref[l_valid:l_valid + HIST, :]
    xpa_ref[0:HIST, :] = xpa_ref[c:c + HIST, :]

    xpb_ref[HIST:HIST + c, :] = u1_ref[:, 0:768]
    conv_b = jnp.zeros((c, 768), F32)
    for jj in range(CONV_B_W):
        off = HIST - (CONV_B_W - 1) + jj
        conv_b = conv_b + xpb_ref[off:off + c, :] * wb_ref[jj:jj + 1, :]
    xpb_ref[0:HIST, :] = xpb_ref[c:c + HIST, :]
    conv_b = _silu(conv_b)
    q_raw = conv_b[:, 0:256]
    k_raw = conv_b[:, 256:512]
    v = conv_b[:, 512:768]
    q = q_raw * lax.rsqrt(_segsum(q_raw * q_raw, bd_bf) + RMS_EPS) * (HEAD_DIM ** -0.5)
    k = k_raw * lax.rsqrt(_segsum(k_raw * k_raw, bd_bf) + RMS_EPS)
    beta = mask_rows(jax.nn.sigmoid(ubb_ref[...]))
    neg_a = prm_ref[0:1, :]
    dt_bias = prm_ref[1:2, :]
    g = mask_rows(neg_a * _softplus(u1_ref[:, 768:1024] + dt_bias))

    gcol = _dot_c3(lincl_bf, g)
    grow = _dot_c3(ones_bf, g * ucat_ref[...])
    decay = jnp.exp(jnp.where(incl, gcol - grow, NEG))
    k_bf = k.astype(BF16)
    kb_bf = blockdiag(k_bf)
    kk = _dot_nt(k_bf, kb_bf)
    qk = _dot_nt(q.astype(BF16), kb_bf)
    a_mat = jnp.where(strict, kk * decay * beta, 0.0)
    p_mat = qk * decay

    def mmc(x, y):
        xh, xl = _split2(x)
        yh, yl = _split2(y)
        yhb = blockdiag(yh)
        return _dot(xh, yhb) + _dot(xl, yhb) + _dot(xh, blockdiag(yl))

    t_mat = eye - a_mat
    pw = a_mat
    for _ in range(5):
        pw = mmc(pw, pw)
        t_mat = t_mat + mmc(pw, t_mat)

    egc = jnp.exp(gcol)
    ruh, rul = _split2(v * beta)
    rwh, rwl = _split2(k * beta * egc)
    rh = jnp.concatenate([blockdiag(ruh), blockdiag(rwh)], axis=1)
    rl = jnp.concatenate([blockdiag(rul), blockdiag(rwl)], axis=1)
    th, tl = _split2(t_mat)
    sol = _dot(th, rh) + _dot(tl, rh) + _dot(th, rl)
    u_sol = sol[:, 0:256]
    w_sol = sol[:, 256:512]

    sg = sg_ref[...]
    sg_bf = sg.astype(BF16)
    v_new = u_sol - _dot(w_sol.astype(BF16), sg_bf)
    o_b = _dot((q * egc).astype(BF16), sg_bf) + _dot(p_mat.astype(BF16), blockdiag(v_new.astype(BF16)))
    glast = gcol[c - 1:c, :]
    kd = k * jnp.exp(glast - gcol)
    sg_new = sg * jnp.exp(glast) + _dot_tn(kd.astype(BF16), v_new.astype(BF16)) * bd
    sg_ref[...] = sg_new
    ms_b = _segsum(o_b * o_b, bd_bf) * (1.0 / HEAD_DIM)
    y_ref[:, 256:512] = o_b * lax.rsqrt(ms_b + RMS_EPS) * prm_ref[2:3, :] * _silu(b_z)

    c_q = u2_ref[:, 0:256]
    c_f = u2_ref[:, 256:512]
    c_i = u2_ref[:, 512:768]
    c_g = u2_ref[:, 768:1024]
    lb = prm_ref[4:5, :]
    sig = jax.nn.sigmoid(c_f)
    log_f = mask_rows(jnp.log(lb + (1.0 - lb) * sig))
    k_c = mask_rows((1.0 - lb) * (1.0 - sig))
    q_c = _silu(c_q)
    bcum = _dot_c3(lincl_bf, log_f)
    sh = sh_ref[...]
    row = lax.broadcasted_iota(jnp.int32, (c, GROUP_W), 0)
    jcol = lax.broadcasted_iota(jnp.int32, (c, GROUP_W), 1) & (c - 1)
    sub = row >> SUB_SHIFT
    row_in_sub = row & (SUB - 1)
    hbp_ref[SUB:SUB + c, :] = bcum
    hkp_ref[SUB:SUB + c, :] = k_c
    anchor = jnp.zeros((c, GROUP_W), F32)
    for s in range(1, c // SUB):
        anchor = jnp.where(sub == s, hbp_ref[SUB + s * SUB - 1:SUB + s * SUB, :], anchor)
    q_anch = (q_c * jnp.exp(bcum - anchor)).astype(BF16)
    att = jnp.zeros((c, GROUP_W), F32)
    for s in range(1, c // SUB):
        a_s = hbp_ref[SUB + s * SUB - 1:SUB + s * SUB, :]
        k_anch = k_c * jnp.exp(jnp.where(row < s * SUB, a_s - bcum, NEG))
        att = att + jnp.where(sub == s, _dot_nt(q_anch, blockdiag(k_anch.astype(BF16))), 0.0)
    for dlt in range(SUB):
        b_sh = hbp_ref[SUB - dlt:SUB - dlt + c, :]
        k_sh = hkp_ref[SUB - dlt:SUB - dlt + c, :]
        e = jnp.exp(jnp.where(row_in_sub >= dlt, bcum - b_sh, NEG))
        gs_ref[dlt * c:(dlt + 1) * c, :] = (q_c * e * k_sh).astype(BF16)
    attb = _dot(gs_ref[...], bd_bf)
    for dlt in range(SUB):
        att = att + jnp.where(jcol == row - dlt, attb[dlt * c:(dlt + 1) * c, :], 0.0)
    o_c = (_dot_nt((q_c * jnp.exp(bcum)).astype(BF16), sh.astype(BF16))
           + _dot(att.astype(BF16), blockdiag(c_i.astype(BF16))))
    blast = bcum[c - 1:c, :]
    kd_c = k_c * jnp.exp(blast - bcum)
    sh_ref[...] = sh * jnp.exp(blast) + _dot_tn(c_i.astype(BF16), kd_c.astype(BF16)) * bd
    ms_c = _segsum(o_c * o_c, bd_bf) * (1.0 / HEAD_DIM)
    y_ref[:, 512:768] = o_c * lax.rsqrt(ms_c + RMS_EPS) * prm_ref[3:4, :] * _silu(c_g)

    @pl.when(ci == nc - 1)
    def _():
        sg_out_ref[0] = sg_ref[...]
        sh_out_ref[0] = sh_ref[...]


def _recur(u, l_valid, hist_a, hist_b, sg0, sh0, wa, wb, prm):
    bsz = sg0.shape[0]
    n = u.shape[0]
    nc = n // bsz // CHUNK
    consts = _recur_consts()
    cnames = ("lincl", "ucat", "incl", "strict", "eye", "bd")
    cvals = [jnp.asarray(consts[k]) for k in cnames]
    row_map = lambda col: (lambda b, ci: (b * nc + ci, col))
    full2 = lambda b, ci: (0, 0)
    per_b = lambda b, ci: (b, 0, 0)
    in_specs = [
        pl.BlockSpec((CHUNK, 1024), row_map(0)),
        pl.BlockSpec((CHUNK, 1024), row_map(1)),
        pl.BlockSpec((CHUNK, 1024), row_map(2)),
        pl.BlockSpec((CHUNK, 256), row_map(C_BB // 256)),
        pl.BlockSpec((1, HIST, 256), per_b),
        pl.BlockSpec((1, HIST, 768), per_b),
        pl.BlockSpec((1, 256, 256), per_b),
        pl.BlockSpec((1, 256, 256), per_b),
        pl.BlockSpec(wa.shape, full2),
        pl.BlockSpec(wb.shape, full2),
        pl.BlockSpec(prm.shape, full2),
    ] + [pl.BlockSpec(cv.shape, full2) for cv in cvals]
    out_shape = (jax.ShapeDtypeStruct((n, 768), F32),
                 jax.ShapeDtypeStruct((bsz, 256, 256), F32),
                 jax.ShapeDtypeStruct((bsz, 256, 256), F32),
                 jax.ShapeDtypeStruct((bsz, HIST, 256), F32))
    out_specs = (pl.BlockSpec((CHUNK, 768), lambda b, ci: (b * nc + ci, 0)),
                 pl.BlockSpec((1, 256, 256), per_b),
                 pl.BlockSpec((1, 256, 256), per_b),
                 pl.BlockSpec((1, HIST, 256), per_b))
    scratch = [pltpu.VMEM((256, 256), F32), pltpu.VMEM((256, 256), F32),
               pltpu.VMEM((CHUNK + HIST, 256), F32), pltpu.VMEM((CHUNK + HIST, 768), F32),
               pltpu.VMEM((SUB + CHUNK, 256), F32), pltpu.VMEM((SUB + CHUNK, 256), F32),
               pltpu.VMEM((SUB * CHUNK, 256), BF16)]
    return pl.pallas_call(
        functools.partial(_recur_kernel, l_valid),
        out_shape=out_shape,
        grid=(bsz, nc),
        in_specs=in_specs,
        out_specs=out_specs,
        scratch_shapes=scratch,
        compiler_params=pltpu.CompilerParams(
            dimension_semantics=("parallel", "arbitrary"), vmem_limit_bytes=VMEM_LIMIT),
        name="recur",
    )(u, u, u, u, hist_a, hist_b, sg0, sh0, wa, wb, prm, *cvals)


def _t5_bucket_np(dist):
    n = np.maximum(dist, 0)
    nf = np.maximum(n, MAX_EXACT).astype(np.float32)
    ratio = (np.log(nf / np.float32(MAX_EXACT)) / np.float32(math.log(MAX_DISTANCE / MAX_EXACT))).astype(np.float32)
    large = MAX_EXACT + (ratio * np.float32(N_BUCKETS - MAX_EXACT)).astype(np.int32)
    large = np.minimum(large, N_BUCKETS - 1)
    return np.where(n < MAX_EXACT, n, large).astype(np.int32)


def _bias_from_buckets(bkt, relb_ref, h):
    out = jnp.full(bkt.shape, NEG, F32)
    for b in range(N_BUCKETS):
        out = jnp.where(bkt == b, relb_ref[b, h], out)
    return out


def _head_lane_mask(h, shape):
    lane = lax.broadcasted_iota(jnp.int32, shape, len(shape) - 1)
    return (lane >= h * HEAD_DIM) & (lane < (h + 1) * HEAD_DIM)


def _map_lane_mask(h, m, shape):
    lane = lax.broadcasted_iota(jnp.int32, shape, len(shape) - 1)
    lo = h * HEAD_DIM + m * HEAD_DIM_DQK
    return (lane >= lo) & (lane < lo + HEAD_DIM_DQK)


def _attn_finish(o1, o2, lam_ref, nd_ref, bd_bf):
    od = o1 - lam_ref[...] * o2
    ms = _segsum(od * od, bd_bf) * (1.0 / HEAD_DIM)
    return od * lax.rsqrt(ms + RMS_EPS) * nd_ref[...]


def _attn_prompt_kernel(relb_ref, q_ref, k_ref, v_ref, bkt_ref, lam_ref, nd_ref, bd_ref, o_ref,
                        q8_ref, m_ref, l_ref, acc_ref, bias_ref):
    b = pl.program_id(0)
    i = pl.program_id(1)
    j = pl.program_id(2)
    tq = q_ref.shape[0]

    @pl.when((b == 0) & (i == 0) & (j == 0))
    def _():
        for d in range(3):
            bkt = bkt_ref[d]
            for h in range(N_HEADS):
                bias_ref[d, h] = _bias_from_buckets(bkt, relb_ref, h)

    @pl.when(j == 0)
    def _():
        qt = (q_ref[...] * (HEAD_DIM_DQK ** -0.5)).T
        feat = lax.broadcasted_iota(jnp.int32, qt.shape, 0)
        for hm in range(2 * N_HEADS):
            lo = hm * HEAD_DIM_DQK
            q8_ref[hm] = jnp.where((feat >= lo) & (feat < lo + HEAD_DIM_DQK), qt, 0.0).astype(BF16)
        m_ref[...] = jnp.full(m_ref.shape, NEG, F32)
        l_ref[...] = jnp.zeros(l_ref.shape, F32)
        acc_ref[...] = jnp.zeros(acc_ref.shape, F32)

    @pl.when(j <= i)
    def _():
        kb = k_ref[...].astype(BF16)
        vt = v_ref[...].T.astype(BF16)
        d = jnp.minimum(i - j, 2)
        for hm in range(2 * N_HEADS):
            h = hm // 2
            s = _dot(kb, q8_ref[hm]) + bias_ref[d, h]
            m_old = m_ref[hm]
            m_new = jnp.maximum(m_old, jnp.max(s, axis=0, keepdims=True))
            p = jnp.exp(s - m_new)
            alpha = jnp.exp(m_old - m_new)
            l_ref[hm] = alpha * l_ref[hm] + jnp.sum(p, axis=0, keepdims=True)
            acc_ref[hm] = alpha * acc_ref[hm] + _dot(vt[h * HEAD_DIM:(h + 1) * HEAD_DIM, :], p.astype(BF16))
            m_ref[hm] = m_new

    @pl.when(j == i)
    def _():
        o1 = jnp.concatenate([acc_ref[2 * h] / l_ref[2 * h] for h in range(N_HEADS)], axis=0)
        o2 = jnp.concatenate([acc_ref[2 * h + 1] / l_ref[2 * h + 1] for h in range(N_HEADS)], axis=0)
        o_ref[...] = _attn_finish(o1.T, o2.T, lam_ref, nd_ref, bd_ref[...].astype(BF16))


def _attn_prompt(u, bsz, seq, rel_bias, lam_row, nd_row, bd):
    t = min(256, seq)
    nq = seq // t
    dist = [d * t + np.arange(t)[None, :] - np.arange(t)[:, None] for d in range(3)]
    bkt = np.stack([np.where(dd >= 0, _t5_bucket_np(dd), -1) for dd in dist]).astype(np.int32)
    assert nq <= 2 or t >= MAX_DISTANCE, "blocks two or more tiles back must share the last bucket"
    full2 = lambda b, i, j: (0, 0)
    return pl.pallas_call(
        _attn_prompt_kernel,
        out_shape=jax.ShapeDtypeStruct((bsz * seq, GROUP_W), F32),
        grid=(bsz, nq, nq),
        in_specs=[
            pl.BlockSpec(memory_space=pltpu.SMEM),
            pl.BlockSpec((t, GROUP_W), lambda b, i, j: (b * nq + i, C_DQ // 256)),
            pl.BlockSpec((t, GROUP_W), lambda b, i, j: (b * nq + jnp.minimum(j, i), C_DK // 256)),
            pl.BlockSpec((t, GROUP_W), lambda b, i, j: (b * nq + jnp.minimum(j, i), C_DV // 256)),
            pl.BlockSpec((3, t, t), lambda b, i, j: (0, 0, 0)),
            pl.BlockSpec((1, GROUP_W), full2),
            pl.BlockSpec((1, GROUP_W), full2),
            pl.BlockSpec((GROUP_W, GROUP_W), full2),
        ],
        out_specs=pl.BlockSpec((t, GROUP_W), lambda b, i, j: (b * nq + i, 0)),
        scratch_shapes=[pltpu.VMEM((8, GROUP_W, t), BF16), pltpu.VMEM((8, 1, t), F32),
                        pltpu.VMEM((8, 1, t), F32), pltpu.VMEM((8, HEAD_DIM, t), F32),
                        pltpu.VMEM((3, N_HEADS, t, t), F32)],
        compiler_params=pltpu.CompilerParams(
            dimension_semantics=("arbitrary", "arbitrary", "arbitrary"), vmem_limit_bytes=VMEM_LIMIT),
        name="attn_prompt",
    )(rel_bias, u, u, u, jnp.asarray(bkt), lam_row, nd_row, bd)


def _attn_sample_kernel(pp, n_steps, pt_ref, relb_ref, q_ref, kn_ref, vn_ref, *rest):
    k_refs = rest[:pp]
    v_refs = rest[pp:2 * pp]
    (bktl_ref, bktn_ref, lam_ref, nd_ref, o_ref,
     q16_ref, m_ref, l_ref, acc_ref, bias_ref, biasn_ref) = rest[2 * pp:]
    b = pl.program_id(0)
    s_id = pl.program_id(1)
    rows = 8
    page = k_refs[0].shape[-1]

    @pl.when((b == 0) & (s_id == 0))
    def _():
        for h in range(N_HEADS):
            near = _bias_from_buckets(bktl_ref[...], relb_ref, h)
            far = jnp.full((rows, page), relb_ref[N_BUCKETS - 1, h], F32)
            newb = _bias_from_buckets(bktn_ref[...], relb_ref, h)
            for m in range(2):
                r0 = m * rows
                for t in range(pp):
                    bias_ref[0, h, r0:r0 + rows, t * page:(t + 1) * page] = far
                    bias_ref[1, h, r0:r0 + rows, t * page:(t + 1) * page] = near if t == pp - 1 else far
                biasn_ref[h, r0:r0 + rows, :] = newb

    @pl.when(s_id == 0)
    def _():
        lane = lax.broadcasted_iota(jnp.int32, (rows, HEAD_DIM), 1)
        for h in range(N_HEADS):
            qh = q_ref[0, h] * (HEAD_DIM_DQK ** -0.5)
            q16_ref[h, 0:rows, :] = jnp.where(lane < HEAD_DIM_DQK, qh, 0.0)
            q16_ref[h, rows:2 * rows, :] = jnp.where(lane >= HEAD_DIM_DQK, qh, 0.0)
            s = _dot_nt(q16_ref[h].astype(BF16), kn_ref[0, h].astype(BF16)) + biasn_ref[h]
            m0 = jnp.max(s, axis=1, keepdims=True)
            p = jnp.exp(s - m0)
            m_ref[h] = m0
            l_ref[h] = jnp.sum(p, axis=1, keepdims=True)
            acc_ref[h] = _dot(p.astype(BF16), vn_ref[0, h].astype(BF16))

    sel = jnp.where(s_id == n_steps - 1, 1, 0)
    for h in range(N_HEADS):
        kt = jnp.concatenate([r[0, 0, h] for r in k_refs], axis=1).astype(BF16)
        vt = jnp.concatenate([r[0, 0, h] for r in v_refs], axis=1).astype(BF16)
        s = _dot(q16_ref[h].astype(BF16), kt) + bias_ref[sel, h]
        m_old = m_ref[h]
        m_new = jnp.maximum(m_old, jnp.max(s, axis=1, keepdims=True))
        p = jnp.exp(s - m_new)
        alpha = jnp.exp(m_old - m_new)
        l_ref[h] = alpha * l_ref[h] + jnp.sum(p, axis=1, keepdims=True)
        acc_ref[h] = alpha * acc_ref[h] + _dot_nt(p.astype(BF16), vt)
        m_ref[h] = m_new

    @pl.when(s_id == n_steps - 1)
    def _():
        for h in range(N_HEADS):
            o = acc_ref[h] / l_ref[h]
            od = o[0:rows, :] - lam_ref[...] * o[rows:2 * rows, :]
            ms = jnp.mean(od * od, axis=1, keepdims=True)
            o_ref[0, h] = od * lax.rsqrt(ms + RMS_EPS) * nd_ref[...]


def _attn_sample(layer, q4, k4, v4, ck, cv, page_table, l_new, rel_bias, lam_row, nd_row):
    bsz = q4.shape[0]
    n_pages = page_table.shape[1]
    page = ck.shape[-1]
    pp = 8
    while n_pages % pp:
        pp //= 2
    n_steps = n_pages // pp
    past = n_pages * page
    qpos = past + np.arange(8)[:, None]
    bkt_last = _t5_bucket_np(qpos - (past - page + np.arange(page))[None, :])
    dist_new = np.arange(8)[:, None] - np.arange(8)[None, :]
    ok = (dist_new >= 0) & (np.arange(8)[None, :] < l_new)
    bkt_new = np.where(ok, _t5_bucket_np(dist_new), -1).astype(np.int32)
    assert page >= MAX_DISTANCE, "only the last cache page may need distance-dependent bias"

    def page_map(t):
        return lambda b, s, pt: (layer, pt[b, s * pp + t], 0, 0, 0)

    per_b = lambda b, s, pt: (b, 0, 0, 0)
    full2 = lambda b, s, pt: (0, 0)
    new_spec = pl.BlockSpec((1, N_HEADS, 8, HEAD_DIM), per_b)
    page_spec = lambda t: pl.BlockSpec((1, 1, N_HEADS, HEAD_DIM, page), page_map(t))
    in_specs = ([pl.BlockSpec(memory_space=pltpu.SMEM), new_spec, new_spec, new_spec]
                + [page_spec(t) for t in range(pp)]
                + [page_spec(t) for t in range(pp)]
                + [pl.BlockSpec((8, page), full2), pl.BlockSpec((8, 8), full2),
                   pl.BlockSpec((1, HEAD_DIM), full2), pl.BlockSpec((1, HEAD_DIM), full2)])
    grid_spec = pltpu.PrefetchScalarGridSpec(
        num_scalar_prefetch=1,
        grid=(bsz, n_steps),
        in_specs=in_specs,
        out_specs=new_spec,
        scratch_shapes=[pltpu.VMEM((N_HEADS, 16, HEAD_DIM), F32), pltpu.VMEM((N_HEADS, 16, 1), F32),
                        pltpu.VMEM((N_HEADS, 16, 1), F32), pltpu.VMEM((N_HEADS, 16, HEAD_DIM), F32),
                        pltpu.VMEM((2, N_HEADS, 16, pp * page), F32), pltpu.VMEM((N_HEADS, 16, 8), F32)],
    )
    return pl.pallas_call(
        functools.partial(_attn_sample_kernel, pp, n_steps),
        out_shape=jax.ShapeDtypeStruct((bsz, N_HEADS, 8, HEAD_DIM), F32),
        grid_spec=grid_spec,
        compiler_params=pltpu.CompilerParams(
            dimension_semantics=("arbitrary", "arbitrary"), vmem_limit_bytes=VMEM_LIMIT),
        name="attn_sample",
    )(page_table, rel_bias, q4, k4, v4, *([ck] * pp), *([cv] * pp),
      jnp.asarray(bkt_last), jnp.asarray(bkt_new), lam_row[:, :HEAD_DIM], nd_row[:, :HEAD_DIM])


def _oproj_kernel(alpha, yabc_ref, yd_ref, x_ref, wo_ref, g_ref, b_ref, o_ref):
    mix = (_dot(yabc_ref[...].astype(BF16), wo_ref[0:768, :])
           + _dot(yd_ref[...].astype(BF16), wo_ref[768:1024, :]))
    o_ref[...] = _layer_norm(alpha * x_ref[...] + mix, g_ref[...], b_ref[...])


def _oproj(yabc, yd, x, wo, g, b, alpha, tm):
    n = x.shape[0]
    row = lambda i: (i, 0)
    full = lambda i: (0, 0)
    return pl.pallas_call(
        functools.partial(_oproj_kernel, alpha),
        out_shape=jax.ShapeDtypeStruct((n, D_MODEL), F32),
        grid=(n // tm,),
        in_specs=[pl.BlockSpec((tm, 768), row), pl.BlockSpec((tm, 256), row),
                  pl.BlockSpec((tm, D_MODEL), row), pl.BlockSpec((D_MODEL, D_MODEL), full),
                  pl.BlockSpec((1, D_MODEL), full), pl.BlockSpec((1, D_MODEL), full)],
        out_specs=pl.BlockSpec((tm, D_MODEL), row),
        compiler_params=pltpu.CompilerParams(
            dimension_semantics=("parallel",), vmem_limit_bytes=VMEM_LIMIT),
        name="out_proj_ln",
    )(yabc, yd, x, wo, g, b)


def _ffn_kernel(alpha, moe, x_ref, wr_ref, wg_ref, wu_ref, wd_ref, g_ref, b_ref, o_ref,
                xb_ref, acc_ref, comb_ref):
    e = pl.program_id(1)
    f = pl.program_id(2)
    last = (e == pl.num_programs(1) - 1) & (f == pl.num_programs(2) - 1)

    @pl.when((e == 0) & (f == 0))
    def _():
        x = x_ref[...]
        xb_ref[...] = x.astype(BF16)
        acc_ref[...] = jnp.zeros(acc_ref.shape, F32)
        if moe:
            logits = jnp.dot(x, wr_ref[...], preferred_element_type=F32, precision=lax.Precision.HIGHEST)
            lane = lax.broadcasted_iota(jnp.int32, logits.shape, 1)
            logits = jnp.where(lane < N_EXPERTS, logits, NEG)
            big = logits.shape[1]
            m1 = jnp.max(logits, axis=1, keepdims=True)
            i1 = jnp.min(jnp.where(logits == m1, lane, big), axis=1, keepdims=True)
            rest = jnp.where(lane == i1, NEG, logits)
            m2 = jnp.max(rest, axis=1, keepdims=True)
            i2 = jnp.min(jnp.where(rest == m2, lane, big), axis=1, keepdims=True)
            e2 = jnp.exp(m2 - m1)
            g1 = 1.0 / (1.0 + e2)
            g2 = e2 / (1.0 + e2)
            comb_ref[...] = jnp.where(lane == i1, g1, 0.0) + jnp.where(lane == i2, g2, 0.0)

    xb = xb_ref[...]
    h = _silu(_dot(xb, wg_ref[0])) * _dot(xb, wu_ref[0])
    if moe:
        lane = lax.broadcasted_iota(jnp.int32, comb_ref.shape, 1)
        h = h * jnp.sum(jnp.where(lane == e, comb_ref[...], 0.0), axis=1, keepdims=True)
    acc_ref[...] += _dot(h.astype(BF16), wd_ref[0])

    @pl.when(last)
    def _():
        o_ref[...] = _layer_norm(alpha * x_ref[...] + acc_ref[...], g_ref[...], b_ref[...])


def _ffn(x, wr, wg, wu, wd, g, b, alpha, moe, tm, tf):
    n = x.shape[0]
    n_e, _, ff = wg.shape
    row = lambda i, e, f: (i, 0)
    full = lambda i, e, f: (0, 0)
    return pl.pallas_call(
        functools.partial(_ffn_kernel, alpha, moe),
        out_shape=jax.ShapeDtypeStruct((n, D_MODEL), F32),
        grid=(n // tm, n_e, ff // tf),
        in_specs=[pl.BlockSpec((tm, D_MODEL), row),
                  pl.BlockSpec(wr.shape, full),
                  pl.BlockSpec((1, D_MODEL, tf), lambda i, e, f: (e, 0, f)),
                  pl.BlockSpec((1, D_MODEL, tf), lambda i, e, f: (e, 0, f)),
                  pl.BlockSpec((1, tf, D_MODEL), lambda i, e, f: (e, f, 0)),
                  pl.BlockSpec((1, D_MODEL), full), pl.BlockSpec((1, D_MODEL), full)],
        out_specs=pl.BlockSpec((tm, D_MODEL), row),
        scratch_shapes=[pltpu.VMEM((tm, D_MODEL), BF16), pltpu.VMEM((tm, D_MODEL), F32),
                        pltpu.VMEM((tm, 128), F32)],
        compiler_params=pltpu.CompilerParams(
            dimension_semantics=("parallel", "arbitrary", "arbitrary"), vmem_limit_bytes=VMEM_LIMIT),
        name="moe_ln" if moe else "ffn_ln",
    )(x, wr, wg, wu, wd, g, b)


def _w_in_columns():
    o = {}
    off = 0
    for name, size in (("a_in", 256), ("a_gb", 256), ("a_gc", 256), ("b_qkv", 768), ("b_a", 4), ("b_b", 4),
                       ("b_z", 256), ("c_q", 256), ("c_f", 256), ("c_i", 256), ("c_g", 256),
                       ("d_q", 256), ("d_k", 256), ("d_v", 256)):
        o[name] = np.arange(off, off + size)
        off += size
    rep = lambda ix: np.repeat(ix, HEAD_DIM)
    cols = np.concatenate([o["a_in"], o["a_gb"], o["a_gc"], o["b_z"],
                           o["b_qkv"], rep(o["b_a"]),
                           o["c_q"], o["c_f"], o["c_i"], o["c_g"],
                           rep(o["b_b"]), o["d_q"], o["d_k"], o["d_v"]])
    assert cols.shape[0] == U_W
    return cols


def _pick_tile(n, pref):
    t = min(pref, n)
    while n % t:
        t //= 2
    return t


def _blockdiag_state(s):
    bsz = s.shape[0]
    eye = jnp.eye(N_HEADS, dtype=s.dtype)
    return (s[:, :, :, None, :] * eye[None, :, None, :, None]).reshape(bsz, GROUP_W, GROUP_W)


def _diag_blocks(s):
    bsz = s.shape[0]
    s5 = s.reshape(bsz, N_HEADS, HEAD_DIM, N_HEADS, HEAD_DIM)
    return jnp.stack([s5[:, h, :, h, :] for h in range(N_HEADS)], axis=1)


def _run_group(x, init, attend, lw, depth):
    bsz, seq, _ = x.shape
    lp = -(-seq // CHUNK) * CHUNK
    assert (lp == seq or seq < CHUNK) and seq >= CONV_B_W - 1
    n = bsz * seq
    alpha = (2 * depth) ** 0.25
    xf = x.reshape(n, D_MODEL)
    tm = _pick_tile(n, 512)
    outs = []
    for l in range(depth):
        w = lw[l]
        u = _proj(xf, w["w_in"], tm, 1024)
        u3 = u.reshape(bsz, seq, U_W)
        if init is None:
            hist_a = jnp.zeros((bsz, HIST, 256), F32)
            hist_b = jnp.zeros((bsz, HIST, 768), F32)
            sg0 = jnp.zeros((bsz, GROUP_W, GROUP_W), F32)
            sh0 = sg0
        else:
            ca, cb, s_g, s_h = init[l]
            hist_a = jnp.pad(ca, ((0, 0), (HIST - ca.shape[1], 0), (0, 0)))
            hist_b = jnp.pad(cb, ((0, 0), (HIST - cb.shape[1], 0), (0, 0)))
            sg0 = _blockdiag_state(s_g)
            sh0 = _blockdiag_state(jnp.swapaxes(s_h, 2, 3))
        u_rec = u if lp == seq else jnp.pad(u3, ((0, 0), (0, lp - seq), (0, 0))).reshape(bsz * lp, U_W)
        yabc, sg1, sh1, tail_a = _recur(u_rec, min(seq, CHUNK) if lp != seq else CHUNK,
                                        hist_a, hist_b, sg0, sh0, w["conv_a"], w["conv_b"], w["prm"])
        if lp != seq:
            yabc = yabc.reshape(bsz, lp, 768)[:, :seq].reshape(n, 768)
        yd = attend(l, u, u3, w)
        x1 = _oproj(yabc, yd, xf, w["w_o"], w["ln1_g"], w["ln1_b"], alpha, tm)
        f = w["ffn"]
        xf = _ffn(x1, f["wr"], f["wg"], f["wu"], f["wd"], w["ln2_g"], w["ln2_b"], alpha, f["moe"],
                  _pick_tile(n, 512), f["tf"])
        a0 = HIST - (CONV_A_W - 1)
        outs.append((
            u3[:, :, C_DK:C_DK + 256].reshape(bsz, seq, N_HEADS, HEAD_DIM),
            u3[:, :, C_DV:C_DV + 256].reshape(bsz, seq, N_HEADS, HEAD_DIM),
            tail_a[:, a0:a0 + CONV_A_W - 1],
            u3[:, seq - (CONV_B_W - 1):, C_BQKV:C_BQKV + 768],
            _diag_blocks(sg1),
            jnp.swapaxes(_diag_blocks(sh1), 2, 3),
        ))
    return xf.reshape(bsz, seq, D_MODEL), [jnp.stack([o[i] for o in outs]) for i in range(6)]


def kernel(x_prompt, x_sample, cache_k, cache_v, state_conv_a, state_conv_b, state_gdn, state_hgrn, page_table, w_in, conv_a, conv_b, gdn_a_log, gdn_dt_bias, norm_b, lower_bounds, norm_c, lambda_q1, lambda_k1, lambda_q2, lambda_k2, norm_d, rel_bias, w_o, ln1_g, ln1_b, ffn_w_gate, ffn_w_up, ffn_w_down, router_w, moe_w_gate, moe_w_up, moe_w_down, ln2_g, ln2_b):
    depth = w_in.shape[0]
    cols = _w_in_columns()
    rep = lambda t: jnp.repeat(t.astype(F32), HEAD_DIM)
    tile4 = lambda t: jnp.tile(t.astype(F32), N_HEADS)
    lbs = jax.nn.softmax(lower_bounds.astype(F32), axis=0)
    lb_all = jnp.cumsum(lbs, axis=0) - lbs[0]
    bd = jnp.asarray(_recur_consts()["bd"])
    lw = []
    for l in range(depth):
        lam_init = 0.8 - 0.6 * math.exp(-0.3 * l)
        lam = (jnp.exp(jnp.sum(lambda_q1[l].astype(F32) * lambda_k1[l].astype(F32)))
               - jnp.exp(jnp.sum(lambda_q2[l].astype(F32) * lambda_k2[l].astype(F32))) + lam_init)
        prm = jnp.stack([-jnp.exp(rep(gdn_a_log[l])), rep(gdn_dt_bias[l]), tile4(norm_b[l]),
                         tile4(norm_c[l]), lb_all[l], jnp.zeros((GROUP_W,), F32),
                         jnp.zeros((GROUP_W,), F32), jnp.zeros((GROUP_W,), F32)])
        j = l // 2
        if l % 2 == 0:
            ff = ffn_w_gate.shape[2]
            ffn = dict(moe=False, wr=jnp.zeros((D_MODEL, 128), F32),
                       wg=ffn_w_gate[j][None].astype(BF16), wu=ffn_w_up[j][None].astype(BF16),
                       wd=ffn_w_down[j][None].astype(BF16))
        else:
            ff = moe_w_gate.shape[3]
            ffn = dict(moe=True, wr=jnp.pad(router_w[j].astype(F32), ((0, 0), (0, 128 - N_EXPERTS))),
                       wg=moe_w_gate[j].astype(BF16), wu=moe_w_up[j].astype(BF16),
                       wd=moe_w_down[j].astype(BF16))
        tf = ff
        for cand in (1408, 1024, 512, 256, 128):
            if ff % cand == 0:
                tf = cand
                break
        ffn["tf"] = tf
        lw.append(dict(
            w_in=w_in[l][:, cols].astype(BF16),
            conv_a=jnp.pad(conv_a[l].astype(F32), ((0, HIST - CONV_A_W), (0, 0))),
            conv_b=jnp.pad(conv_b[l].astype(F32), ((0, HIST - CONV_B_W), (0, 0))),
            prm=prm,
            lam_row=jnp.full((1, GROUP_W), lam, F32),
            nd_row=(tile4(norm_d[l]) * (1.0 - lam_init))[None, :],
            w_o=w_o[l].astype(BF16),
            ln1_g=ln1_g[l][None].astype(F32), ln1_b=ln1_b[l][None].astype(F32),
            ln2_g=ln2_g[l][None].astype(F32), ln2_b=ln2_b[l][None].astype(F32),
            ffn=ffn,
        ))
    relb = rel_bias.astype(F32)

    def attend_prompt(l, u, u3, w):
        bsz, seq, _ = u3.shape
        return _attn_prompt(u, bsz, seq, relb, w["lam_row"], w["nd_row"], bd)

    ck_t = jnp.transpose(cache_k, (0, 1, 3, 4, 2))
    cv_t = jnp.transpose(cache_v, (0, 1, 3, 4, 2))

    def attend_sample(l, u, u3, w):
        bsz, seq, _ = u3.shape

        def heads8(c0):
            t = u3[:, :, c0:c0 + 256].reshape(bsz, seq, N_HEADS, HEAD_DIM)
            return jnp.pad(jnp.swapaxes(t, 1, 2), ((0, 0), (0, 0), (0, 8 - seq), (0, 0)))

        o = _attn_sample(l, heads8(C_DQ), heads8(C_DK), heads8(C_DV), ck_t, cv_t,
                         page_table, seq, relb, w["lam_row"], w["nd_row"])
        return jnp.swapaxes(o[:, :, :seq], 1, 2).reshape(bsz * seq, GROUP_W)

    init_s = [(state_conv_a[l], state_conv_b[l], state_gdn[l], state_hgrn[l]) for l in range(depth)]
    y_p, (k_p, v_p, ca_p, cb_p, sg_p, sh_p) = _run_group(x_prompt, None, attend_prompt, lw, depth)
    y_s, (k_s, v_s, ca_s, cb_s, sg_s, sh_s) = _run_group(x_sample, init_s, attend_sample, lw, depth)
    return (y_p, y_s, k_p, v_p, k_s, v_s, ca_p, ca_s, cb_p, cb_s, sg_p, sg_s, sh_p, sh_s)
```

```python
import functools
import math

import numpy as np
import jax
import jax.numpy as jnp
from jax import lax
from jax.experimental import pallas as pl
from jax.experimental.pallas import tpu as pltpu

F32 = jnp.float32
BF16 = jnp.bfloat16

D_MODEL = 1024
GROUP_W = 256
N_HEADS = 4
HEAD_DIM = 64
CHUNK = 64
HIST = 8
SUB = 16
SUB_SHIFT = 4
CONV_A_W = 3
CONV_B_W = 4
HEAD_DIM_DQK = 32
N_BUCKETS = 32
MAX_EXACT = 16
MAX_DISTANCE = 128
N_EXPERTS = 8
LN_EPS = 1e-5
RMS_EPS = 1e-6
NEG = -1e30
LOG2E = math.log2(math.e)
QK_SCALE_LOG2 = HEAD_DIM_DQK ** -0.5 * LOG2E
U_W = 4096
VMEM_LIMIT = 56 * 1024 * 1024

C_AIN, C_AGB, C_AGC, C_BZ = 0, 256, 512, 768
C_BQKV, C_BA = 1024, 1792
C_CQ, C_CF, C_CI, C_CG = 2048, 2304, 2560, 2816
C_BB, C_DQ, C_DK, C_DV = 3072, 3328, 3584, 3840


def _dot(a, b):
    return jnp.dot(a, b, preferred_element_type=F32)


def _dot_nt(a, b):
    return lax.dot_general(a, b, (((1,), (1,)), ((), ())), preferred_element_type=F32)


def _dot_tn(a, b):
    return lax.dot_general(a, b, (((0,), (0,)), ((), ())), preferred_element_type=F32)


def _split2(x):
    hi = x.astype(BF16)
    lo = (x - hi.astype(F32)).astype(BF16)
    return hi, lo


def _split3(x):
    hi = x.astype(BF16)
    r = x - hi.astype(F32)
    mid = r.astype(BF16)
    lo = (r - mid.astype(F32)).astype(BF16)
    return hi, mid, lo


def _dot_c3(c_bf, x):
    hi, mid, lo = _split3(x)
    return _dot(c_bf, hi) + _dot(c_bf, mid) + _dot(c_bf, lo)


def _segsum(s, bd_bf):
    hi, lo = _split2(s)
    return _dot(hi, bd_bf) + _dot(lo, bd_bf)


def _silu(x):
    return x * jax.nn.sigmoid(x)


def _softplus(x):
    return jnp.maximum(x, 0.0) + jnp.log1p(jnp.exp(-jnp.abs(x)))


def _layer_norm(z, g, b):
    mu = jnp.mean(z, axis=-1, keepdims=True)
    zc = z - mu
    var = jnp.mean(zc * zc, axis=-1, keepdims=True)
    return zc * lax.rsqrt(var + LN_EPS) * g + b


def _proj_kernel(tn, x_ref, w_ref, o_ref):
    xb = x_ref[...].astype(BF16)
    for c0 in range(0, w_ref.shape[1], tn):
        o_ref[:, c0:c0 + tn] = _dot(xb, w_ref[:, c0:c0 + tn])


def _proj(x, w, tm, tn):
    n, k = x.shape
    m = w.shape[1]
    return pl.pallas_call(
        functools.partial(_proj_kernel, tn),
        out_shape=jax.ShapeDtypeStruct((n, m), F32),
        grid=(n // tm,),
        in_specs=[pl.BlockSpec((tm, k), lambda i: (i, 0)),
                  pl.BlockSpec((k, m), lambda i: (0, 0))],
        out_specs=pl.BlockSpec((tm, m), lambda i: (i, 0)),
        compiler_params=pltpu.CompilerParams(
            dimension_semantics=("parallel",), vmem_limit_bytes=VMEM_LIMIT),
        name="proj_in",
    )(x, w)


def _recur_consts():
    c = CHUNK
    w = GROUP_W
    r = np.arange(c)[:, None]
    lane = np.arange(w)[None, :]
    j = lane % c
    bd = (np.arange(w)[:, None] // c == lane // c)
    return dict(
        lincl=(np.arange(c)[None, :] <= r).astype(np.float32),
        ucat=(r <= j).astype(np.float32),
        incl=(j <= r).astype(np.float32),
        strict=(j < r).astype(np.float32),
        eye=(j == r).astype(np.float32),
        bd=bd.astype(np.float32),
    )


def _recur_kernel(l_valid, u0_ref, u1_ref, u2_ref, ubb_ref, hista_ref, histb_ref, sg0_ref, sh0_ref,
                  wa_ref, wb_ref, prm_ref, lincl_ref, ucat_ref, incl_ref, strict_ref, eye_ref, bd_ref,
                  y_ref, sg_out_ref, sh_out_ref, hista_out_ref,
                  sg_ref, sh_ref, xpa_ref, xpb_ref, hbp_ref, hkp_ref, gs_ref):
    c = CHUNK
    ci = pl.program_id(1)
    nc = pl.num_programs(1)

    @pl.when(ci == 0)
    def _():
        sg_ref[...] = sg0_ref[0]
        sh_ref[...] = sh0_ref[0]
        xpa_ref[0:HIST, :] = hista_ref[0]
        xpb_ref[0:HIST, :] = histb_ref[0]
        hbp_ref[0:SUB, :] = jnp.zeros((SUB, GROUP_W), F32)
        hkp_ref[0:SUB, :] = jnp.zeros((SUB, GROUP_W), F32)

    bd = bd_ref[...]
    bd_bf = bd.astype(BF16)
    lincl_bf = lincl_ref[...].astype(BF16)
    ones_bf = jnp.ones((c, c), BF16)
    incl = incl_ref[...] > 0.5
    strict = strict_ref[...] > 0.5
    eye = eye_ref[...]

    def blockdiag(x):
        return jnp.concatenate([x, x, x, x], axis=0) * (bd_bf if x.dtype == BF16 else bd)

    if l_valid < c:
        valid = lax.broadcasted_iota(jnp.int32, (c, GROUP_W), 0) < l_valid
    else:
        valid = None

    def mask_rows(x):
        return x if valid is None else jnp.where(valid, x, 0.0)

    a_in = u0_ref[:, C_AIN:C_AIN + 256]
    a_gb = u0_ref[:, C_AGB:C_AGB + 256]
    a_gc = u0_ref[:, C_AGC:C_AGC + 256]
    b_z = u0_ref[:, C_BZ:C_BZ + 256]
    xpa_ref[HIST:HIST + c, :] = a_gc * a_in
    conv_a = jnp.zeros((c, GROUP_W), F32)
    for jj in range(CONV_A_W):
        off = HIST - (CONV_A_W - 1) + jj
        conv_a = conv_a + xpa_ref[off:off + c, :] * wa_ref[jj:jj + 1, :]
    y_ref[:, 0:256] = a_gb * conv_a
    hista_out_ref[0] = xpa_ref[l_valid:l_valid + HIST, :]
    xpa_ref[0:HIST, :] = xpa_ref[c:c + HIST, :]

    xpb_ref[HIST:HIST + c, :] = u1_ref[:, 0:768]
    conv_b = jnp.zeros((c, 768), F32)
    for jj in range(CONV_B_W):
        off = HIST - (CONV_B_W - 1) + jj
        conv_b = conv_b + xpb_ref[off:off + c, :] * wb_ref[jj:jj + 1, :]
    xpb_ref[0:HIST, :] = xpb_ref[c:c + HIST, :]
    conv_b = _silu(conv_b)
    q_raw = conv_b[:, 0:256]
    k_raw = conv_b[:, 256:512]
    v = conv_b[:, 512:768]
    q = q_raw * lax.rsqrt(_segsum(q_raw * q_raw, bd_bf) + RMS_EPS) * (HEAD_DIM ** -0.5)
    k = k_raw * lax.rsqrt(_segsum(k_raw * k_raw, bd_bf) + RMS_EPS)
    beta = mask_rows(jax.nn.sigmoid(ubb_ref[...]))
    neg_a = prm_ref[0:1, :]
    dt_bias = prm_ref[1:2, :]
    g = mask_rows(neg_a * _softplus(u1_ref[:, 768:1024] + dt_bias))

    gcol = _dot_c3(lincl_bf, g)
    grow = _dot_c3(ones_bf, g * ucat_ref[...])
    decay = jnp.exp(jnp.where(incl, gcol - grow, NEG))
    k_bf = k.astype(BF16)
    kb_bf = blockdiag(k_bf)
    kk = _dot_nt(k_bf, kb_bf)
    qk = _dot_nt(q.astype(BF16), kb_bf)
    a_mat = jnp.where(strict, kk * decay * beta, 0.0)
    p_mat = qk * decay

    def mmc(x, y):
        return _dot(x.astype(BF16), blockdiag(y.astype(BF16)))

    t_mat = eye - a_mat
    pw = mmc(a_mat, a_mat)
    t_mat = t_mat + mmc(pw, t_mat)

    c_q = u2_ref[:, 0:256]
    c_f = u2_ref[:, 256:512]
    c_i = u2_ref[:, 512:768]
    c_g = u2_ref[:, 768:1024]
    lb = prm_ref[4:5, :]
    sig = jax.nn.sigmoid(c_f)
    log_f = mask_rows(jnp.log(lb + (1.0 - lb) * sig))
    k_c = mask_rows((1.0 - lb) * (1.0 - sig))
    q_c = _silu(c_q)
    bcum = _dot_c3(lincl_bf, log_f)
    sh = sh_ref[...]
    row = lax.broadcasted_iota(jnp.int32, (c, GROUP_W), 0)
    jcol = lax.broadcasted_iota(jnp.int32, (c, GROUP_W), 1) & (c - 1)
    sub = row >> SUB_SHIFT
    row_in_sub = row & (SUB - 1)
    hbp_ref[SUB:SUB + c, :] = bcum
    hkp_ref[SUB:SUB + c, :] = k_c
    anchor = jnp.zeros((c, GROUP_W), F32)
    for s in range(1, c // SUB):
        anchor = jnp.where(sub == s, hbp_ref[SUB + s * SUB - 1:SUB + s * SUB, :], anchor)
    q_anch = (q_c * jnp.exp(bcum - anchor)).astype(BF16)
    att = jnp.zeros((c, GROUP_W), F32)
    for s in range(1, c // SUB):
        a_s = hbp_ref[SUB + s * SUB - 1:SUB + s * SUB, :]
        k_anch = k_c * jnp.exp(jnp.where(row < s * SUB, a_s - bcum, NEG))
        att = att + jnp.where(sub == s, _dot_nt(q_anch, blockdiag(k_anch.astype(BF16))), 0.0)
    pw = mmc(pw, pw)
    t_mat = t_mat + mmc(pw, t_mat)
    for dlt in range(SUB):
        b_sh = hbp_ref[SUB - dlt:SUB - dlt + c, :]
        k_sh = hkp_ref[SUB - dlt:SUB - dlt + c, :]
        e = jnp.exp(jnp.where(row_in_sub >= dlt, bcum - b_sh, NEG))
        gs_ref[dlt * c:(dlt + 1) * c, :] = (q_c * e * k_sh).astype(BF16)
    attb = _dot(gs_ref[...], bd_bf)
    pw = mmc(pw, pw)
    t_mat = t_mat + mmc(pw, t_mat)
    for dlt in range(SUB):
        att = att + jnp.where(jcol == row - dlt, attb[dlt * c:(dlt + 1) * c, :], 0.0)
    o_c = (_dot_nt((q_c * jnp.exp(bcum)).astype(BF16), sh.astype(BF16))
           + _dot(att.astype(BF16), blockdiag(c_i.astype(BF16))))
    pw = mmc(pw, pw)
    t_mat = t_mat + mmc(pw, t_mat)
    blast = bcum[c - 1:c, :]
    kd_c = k_c * jnp.exp(blast - bcum)
    sh_ref[...] = sh * jnp.exp(blast) + _dot_tn(c_i.astype(BF16), kd_c.astype(BF16)) * bd
    pw = mmc(pw, pw)
    t_mat = t_mat + mmc(pw, t_mat)
    ms_c = _segsum(o_c * o_c, bd_bf) * (1.0 / HEAD_DIM)
    y_ref[:, 512:768] = o_c * lax.rsqrt(ms_c + RMS_EPS) * prm_ref[3:4, :] * _silu(c_g)

    egc = jnp.exp(gcol)
    rhs = jnp.concatenate([blockdiag((v * beta).astype(BF16)),
                           blockdiag((k * beta * egc).astype(BF16))], axis=1)
    sol = _dot(t_mat.astype(BF16), rhs)
    u_sol = sol[:, 0:256]
    w_sol = sol[:, 256:512]
    sg = sg_ref[...]
    sg_bf = sg.astype(BF16)
    v_new = u_sol - _dot(w_sol.astype(BF16), sg_bf)
    o_b = _dot((q * egc).astype(BF16), sg_bf) + _dot(p_mat.astype(BF16), blockdiag(v_new.astype(BF16)))
    glast = gcol[c - 1:c, :]
    kd = k * jnp.exp(glast - gcol)
    sg_ref[...] = sg * jnp.exp(glast) + _dot_tn(kd.astype(BF16), v_new.astype(BF16)) * bd
    ms_b = _segsum(o_b * o_b, bd_bf) * (1.0 / HEAD_DIM)
    y_ref[:, 256:512] = o_b * lax.rsqrt(ms_b + RMS_EPS) * prm_ref[2:3, :] * _silu(b_z)

    @pl.when(ci == nc - 1)
    def _():
        sg_out_ref[0] = sg_ref[...]
        sh_out_ref[0] = sh_ref[...]


def _recur(u, l_valid, hist_a, hist_b, sg0, sh0, wa, wb, prm):
    bsz = sg0.shape[0]
    n = u.shape[0]
    nc = n // bsz // CHUNK
    consts = _recur_consts()
    cnames = ("lincl", "ucat", "incl", "strict", "eye", "bd")
    cvals = [jnp.asarray(consts[k]) for k in cnames]
    row_map = lambda col: (lambda b, ci: (b * nc + ci, col))
    full2 = lambda b, ci: (0, 0)
    per_b = lambda b, ci: (b, 0, 0)
    in_specs = [
        pl.BlockSpec((CHUNK, 1024), row_map(0)),
        pl.BlockSpec((CHUNK, 1024), row_map(1)),
        pl.BlockSpec((CHUNK, 1024), row_map(2)),
        pl.BlockSpec((CHUNK, 256), row_map(C_BB // 256)),
        pl.BlockSpec((1, HIST, 256), per_b),
        pl.BlockSpec((1, HIST, 768), per_b),
        pl.BlockSpec((1, 256, 256), per_b),
        pl.BlockSpec((1, 256, 256), per_b),
        pl.BlockSpec(wa.shape, full2),
        pl.BlockSpec(wb.shape, full2),
        pl.BlockSpec(prm.shape, full2),
    ] + [pl.BlockSpec(cv.shape, full2) for cv in cvals]
    out_shape = (jax.ShapeDtypeStruct((n, 768), F32),
                 jax.ShapeDtypeStruct((bsz, 256, 256), F32),
                 jax.ShapeDtypeStruct((bsz, 256, 256), F32),
                 jax.ShapeDtypeStruct((bsz, HIST, 256), F32))
    out_specs = (pl.BlockSpec((CHUNK, 768), lambda b, ci: (b * nc + ci, 0)),
                 pl.BlockSpec((1, 256, 256), per_b),
                 pl.BlockSpec((1, 256, 256), per_b),
                 pl.BlockSpec((1, HIST, 256), per_b))
    scratch = [pltpu.VMEM((256, 256), F32), pltpu.VMEM((256, 256), F32),
               pltpu.VMEM((CHUNK + HIST, 256), F32), pltpu.VMEM((CHUNK + HIST, 768), F32),
               pltpu.VMEM((SUB + CHUNK, 256), F32), pltpu.VMEM((SUB + CHUNK, 256), F32),
               pltpu.VMEM((SUB * CHUNK, 256), BF16)]
    return pl.pallas_call(
        functools.partial(_recur_kernel, l_valid),
        out_shape=out_shape,
        grid=(bsz, nc),
        in_specs=in_specs,
        out_specs=out_specs,
        scratch_shapes=scratch,
        compiler_params=pltpu.CompilerParams(
            dimension_semantics=("parallel", "arbitrary"), vmem_limit_bytes=VMEM_LIMIT),
        name="recur",
    )(u, u, u, u, hist_a, hist_b, sg0, sh0, wa, wb, prm, *cvals)


def _t5_bucket_np(dist):
    n = np.maximum(dist, 0)
    nf = np.maximum(n, MAX_EXACT).astype(np.float32)
    ratio = (np.log(nf / np.float32(MAX_EXACT)) / np.float32(math.log(MAX_DISTANCE / MAX_EXACT))).astype(np.float32)
    large = MAX_EXACT + (ratio * np.float32(N_BUCKETS - MAX_EXACT)).astype(np.int32)
    large = np.minimum(large, N_BUCKETS - 1)
    return np.where(n < MAX_EXACT, n, large).astype(np.int32)


def _bias_from_buckets(bkt, relb_ref, h):
    out = jnp.full(bkt.shape, NEG, F32)
    for b in range(N_BUCKETS):
        out = jnp.where(bkt == b, relb_ref[b, h] * LOG2E, out)
    return out


def _head_lane_mask(h, shape):
    lane = lax.broadcasted_iota(jnp.int32, shape, len(shape) - 1)
    return (lane >= h * HEAD_DIM) & (lane < (h + 1) * HEAD_DIM)


def _map_lane_mask(h, m, shape):
    lane = lax.broadcasted_iota(jnp.int32, shape, len(shape) - 1)
    lo = h * HEAD_DIM + m * HEAD_DIM_DQK
    return (lane >= lo) & (lane < lo + HEAD_DIM_DQK)


def _attn_finish(o1, o2, lam_ref, nd_ref, bd_bf):
    od = o1 - lam_ref[...] * o2
    ms = _segsum(od * od, bd_bf) * (1.0 / HEAD_DIM)
    return od * lax.rsqrt(ms + RMS_EPS) * nd_ref[...]


def _attn_prompt_kernel(relb_ref, q_ref, k_ref, v_ref, bkt_ref, lam_ref, nd_ref, bd_ref, o_ref,
                        q8_ref, m_ref, l_ref, acc_ref, bias_ref, s_ref, p_ref, al_ref):
    b = pl.program_id(0)
    i = pl.program_id(1)
    j = pl.program_id(2)
    tq = q_ref.shape[0]

    @pl.when((b == 0) & (i == 0) & (j == 0))
    def _():
        for d in range(3):
            bkt = bkt_ref[d]
            for h in range(N_HEADS):
                bias_ref[d, h] = _bias_from_buckets(bkt, relb_ref, h)

    @pl.when(j == 0)
    def _():
        qt = (q_ref[...] * QK_SCALE_LOG2).T
        feat = lax.broadcasted_iota(jnp.int32, qt.shape, 0)
        for hm in range(2 * N_HEADS):
            lo = hm * HEAD_DIM_DQK
            q8_ref[hm] = jnp.where((feat >= lo) & (feat < lo + HEAD_DIM_DQK), qt, 0.0).astype(BF16)
        m_ref[...] = jnp.full(m_ref.shape, NEG, F32)
        l_ref[...] = jnp.zeros(l_ref.shape, F32)
        acc_ref[...] = jnp.zeros(acc_ref.shape, F32)

    @pl.when(j <= i)
    def _():
        kb = k_ref[...].astype(BF16)
        vt = v_ref[...].T.astype(BF16)
        d = jnp.minimum(i - j, 2)
        for hm in range(2 * N_HEADS):
            s_ref[hm] = _dot(kb, q8_ref[hm]) + bias_ref[d, hm // 2]
        for hm in range(2 * N_HEADS):
            s = s_ref[hm]
            m_old = m_ref[hm]
            m_new = jnp.maximum(m_old, jnp.max(s, axis=0, keepdims=True))
            p = jnp.exp2(s - m_new)
            alpha = jnp.exp2(m_old - m_new)
            l_ref[hm] = alpha * l_ref[hm] + jnp.sum(p, axis=0, keepdims=True)
            p_ref[hm] = p.astype(BF16)
            al_ref[hm] = alpha
            m_ref[hm] = m_new
        for hm in range(2 * N_HEADS):
            h = hm // 2
            acc_ref[hm] = al_ref[hm] * acc_ref[hm] + _dot(vt[h * HEAD_DIM:(h + 1) * HEAD_DIM, :], p_ref[hm])

    @pl.when(j == i)
    def _():
        o1 = jnp.concatenate([acc_ref[2 * h] / l_ref[2 * h] for h in range(N_HEADS)], axis=0)
        o2 = jnp.concatenate([acc_ref[2 * h + 1] / l_ref[2 * h + 1] for h in range(N_HEADS)], axis=0)
        o_ref[...] = _attn_finish(o1.T, o2.T, lam_ref, nd_ref, bd_ref[...].astype(BF16))


def _attn_prompt(u, bsz, seq, rel_bias, lam_row, nd_row, bd):
    t = min(256, seq)
    nq = seq // t
    dist = [d * t + np.arange(t)[None, :] - np.arange(t)[:, None] for d in range(3)]
    bkt = np.stack([np.where(dd >= 0, _t5_bucket_np(dd), -1) for dd in dist]).astype(np.int32)
    assert nq <= 2 or t >= MAX_DISTANCE, "blocks two or more tiles back must share the last bucket"
    full2 = lambda b, i, j: (0, 0)
    return pl.pallas_call(
        _attn_prompt_kernel,
        out_shape=jax.ShapeDtypeStruct((bsz * seq, GROUP_W), F32),
        grid=(bsz, nq, nq),
        in_specs=[
            pl.BlockSpec(memory_space=pltpu.SMEM),
            pl.BlockSpec((t, GROUP_W), lambda b, i, j: (b * nq + i, C_DQ // 256)),
            pl.BlockSpec((t, GROUP_W), lambda b, i, j: (b * nq + jnp.minimum(j, i), C_DK // 256)),
            pl.BlockSpec((t, GROUP_W), lambda b, i, j: (b * nq + jnp.minimum(j, i), C_DV // 256)),
            pl.BlockSpec((3, t, t), lambda b, i, j: (0, 0, 0)),
            pl.BlockSpec((1, GROUP_W), full2),
            pl.BlockSpec((1, GROUP_W), full2),
            pl.BlockSpec((GROUP_W, GROUP_W), full2),
        ],
        out_specs=pl.BlockSpec((t, GROUP_W), lambda b, i, j: (b * nq + i, 0)),
        scratch_shapes=[pltpu.VMEM((8, GROUP_W, t), BF16), pltpu.VMEM((8, 1, t), F32),
                        pltpu.VMEM((8, 1, t), F32), pltpu.VMEM((8, HEAD_DIM, t), F32),
                        pltpu.VMEM((3, N_HEADS, t, t), F32), pltpu.VMEM((8, t, t), F32),
                        pltpu.VMEM((8, t, t), BF16), pltpu.VMEM((8, 1, t), F32)],
        compiler_params=pltpu.CompilerParams(
            dimension_semantics=("arbitrary", "arbitrary", "arbitrary"), vmem_limit_bytes=VMEM_LIMIT),
        name="attn_prompt",
    )(rel_bias, u, u, u, jnp.asarray(bkt), lam_row, nd_row, bd)


def _attn_sample_kernel(pp, n_steps, pt_ref, relb_ref, q_ref, kn_ref, vn_ref, *rest):
    k_refs = rest[:pp]
    v_refs = rest[pp:2 * pp]
    (bktl_ref, bktn_ref, lam_ref, nd_ref, o_ref,
     q16_ref, m_ref, l_ref, acc_ref, bias_ref, biasn_ref) = rest[2 * pp:]
    b = pl.program_id(0)
    s_id = pl.program_id(1)
    rows = 8
    page = k_refs[0].shape[-1]

    @pl.when((b == 0) & (s_id == 0))
    def _():
        for h in range(N_HEADS):
            near = _bias_from_buckets(bktl_ref[...], relb_ref, h)
            far = jnp.full((rows, page), relb_ref[N_BUCKETS - 1, h] * LOG2E, F32)
            newb = _bias_from_buckets(bktn_ref[...], relb_ref, h)
            for m in range(2):
                r0 = (2 * h + m) * rows
                for t in range(pp):
                    bias_ref[0, r0:r0 + rows, t * page:(t + 1) * page] = far
                    bias_ref[1, r0:r0 + rows, t * page:(t + 1) * page] = near if t == pp - 1 else far
                biasn_ref[r0:r0 + rows, :] = newb

    hrows = 2 * rows

    @pl.when(s_id == 0)
    def _():
        lane = lax.broadcasted_iota(jnp.int32, (rows, HEAD_DIM), 1)
        for h in range(N_HEADS):
            qh = q_ref[0, h] * QK_SCALE_LOG2
            q16_ref[h * hrows:h * hrows + rows, :] = jnp.where(lane < HEAD_DIM_DQK, qh, 0.0)
            q16_ref[h * hrows + rows:(h + 1) * hrows, :] = jnp.where(lane >= HEAD_DIM_DQK, qh, 0.0)
        s = jnp.concatenate(
            [_dot_nt(q16_ref[h * hrows:(h + 1) * hrows, :].astype(BF16), kn_ref[0, h].astype(BF16))
             for h in range(N_HEADS)], axis=0) + biasn_ref[...]
        m0 = jnp.max(s, axis=1, keepdims=True)
        p = jnp.exp2(s - m0)
        m_ref[...] = m0
        l_ref[...] = jnp.sum(p, axis=1, keepdims=True)
        for h in range(N_HEADS):
            acc_ref[h * hrows:(h + 1) * hrows, :] = _dot(
                p[h * hrows:(h + 1) * hrows, :].astype(BF16), vn_ref[0, h].astype(BF16))

    sel = jnp.where(s_id == n_steps - 1, 1, 0)
    s = jnp.concatenate(
        [_dot(q16_ref[h * hrows:(h + 1) * hrows, :].astype(BF16),
              jnp.concatenate([r[0, 0, h] for r in k_refs], axis=1).astype(BF16))
         for h in range(N_HEADS)], axis=0) + bias_ref[sel]
    m_old = m_ref[...]
    m_new = jnp.maximum(m_old, jnp.max(s, axis=1, keepdims=True))
    p32 = jnp.exp2(s - m_new)
    alpha = jnp.exp2(m_old - m_new)
    l_ref[...] = alpha * l_ref[...] + jnp.sum(p32, axis=1, keepdims=True)
    p = p32.astype(BF16)
    pv = jnp.concatenate(
        [_dot_nt(p[h * hrows:(h + 1) * hrows, :],
                 jnp.concatenate([r[0, 0, h] for r in v_refs], axis=1).astype(BF16))
         for h in range(N_HEADS)], axis=0)
    acc_ref[...] = alpha * acc_ref[...] + pv
    m_ref[...] = m_new

    @pl.when(s_id == n_steps - 1)
    def _():
        o = acc_ref[...] / l_ref[...]
        for h in range(N_HEADS):
            od = o[h * hrows:h * hrows + rows, :] - lam_ref[...] * o[h * hrows + rows:(h + 1) * hrows, :]
            ms = jnp.mean(od * od, axis=1, keepdims=True)
            o_ref[0, h] = od * lax.rsqrt(ms + RMS_EPS) * nd_ref[...]


def _attn_sample(layer, q4, k4, v4, ck, cv, page_table, l_new, rel_bias, lam_row, nd_row):
    bsz = q4.shape[0]
    n_pages = page_table.shape[1]
    page = ck.shape[-1]
    pp = 8
    while n_pages % pp:
        pp //= 2
    n_steps = n_pages // pp
    past = n_pages * page
    qpos = past + np.arange(8)[:, None]
    bkt_last = _t5_bucket_np(qpos - (past - page + np.arange(page))[None, :])
    dist_new = np.arange(8)[:, None] - np.arange(8)[None, :]
    ok = (dist_new >= 0) & (np.arange(8)[None, :] < l_new)
    bkt_new = np.where(ok, _t5_bucket_np(dist_new), -1).astype(np.int32)
    assert page >= MAX_DISTANCE, "only the last cache page may need distance-dependent bias"

    def page_map(t):
        return lambda b, s, pt: (layer, pt[b, s * pp + t], 0, 0, 0)

    per_b = lambda b, s, pt: (b, 0, 0, 0)
    full2 = lambda b, s, pt: (0, 0)
    new_spec = pl.BlockSpec((1, N_HEADS, 8, HEAD_DIM), per_b)
    page_spec = lambda t: pl.BlockSpec((1, 1, N_HEADS, HEAD_DIM, page), page_map(t))
    in_specs = ([pl.BlockSpec(memory_space=pltpu.SMEM), new_spec, new_spec, new_spec]
                + [page_spec(t) for t in range(pp)]
                + [page_spec(t) for t in range(pp)]
                + [pl.BlockSpec((8, page), full2), pl.BlockSpec((8, 8), full2),
                   pl.BlockSpec((1, HEAD_DIM), full2), pl.BlockSpec((1, HEAD_DIM), full2)])
    grid_spec = pltpu.PrefetchScalarGridSpec(
        num_scalar_prefetch=1,
        grid=(bsz, n_steps),
        in_specs=in_specs,
        out_specs=new_spec,
        scratch_shapes=[pltpu.VMEM((N_HEADS * 16, HEAD_DIM), F32), pltpu.VMEM((N_HEADS * 16, 1), F32),
                        pltpu.VMEM((N_HEADS * 16, 1), F32), pltpu.VMEM((N_HEADS * 16, HEAD_DIM), F32),
                        pltpu.VMEM((2, N_HEADS * 16, pp * page), F32), pltpu.VMEM((N_HEADS * 16, 8), F32)],
    )
    return pl.pallas_call(
        functools.partial(_attn_sample_kernel, pp, n_steps),
        out_shape=jax.ShapeDtypeStruct((bsz, N_HEADS, 8, HEAD_DIM), F32),
        grid_spec=grid_spec,
        compiler_params=pltpu.CompilerParams(
            dimension_semantics=("arbitrary", "arbitrary"), vmem_limit_bytes=VMEM_LIMIT),
        name="attn_sample",
    )(page_table, rel_bias, q4, k4, v4, *([ck] * pp), *([cv] * pp),
      jnp.asarray(bkt_last), jnp.asarray(bkt_new), lam_row[:, :HEAD_DIM], nd_row[:, :HEAD_DIM])


def _oproj_kernel(alpha, yabc_ref, yd_ref, x_ref, wo_ref, g_ref, b_ref, o_ref):
    mix = (_dot(yabc_ref[...].astype(BF16), wo_ref[0:768, :])
           + _dot(yd_ref[...].astype(BF16), wo_ref[768:1024, :]))
    o_ref[...] = _layer_norm(alpha * x_ref[...] + mix, g_ref[...], b_ref[...])


def _oproj(yabc, yd, x, wo, g, b, alpha, tm):
    n = x.shape[0]
    row = lambda i: (i, 0)
    full = lambda i: (0, 0)
    return pl.pallas_call(
        functools.partial(_oproj_kernel, alpha),
        out_shape=jax.ShapeDtypeStruct((n, D_MODEL), F32),
        grid=(n // tm,),
        in_specs=[pl.BlockSpec((tm, 768), row), pl.BlockSpec((tm, 256), row),
                  pl.BlockSpec((tm, D_MODEL), row), pl.BlockSpec((D_MODEL, D_MODEL), full),
                  pl.BlockSpec((1, D_MODEL), full), pl.BlockSpec((1, D_MODEL), full)],
        out_specs=pl.BlockSpec((tm, D_MODEL), row),
        compiler_params=pltpu.CompilerParams(
            dimension_semantics=("parallel",), vmem_limit_bytes=VMEM_LIMIT),
        name="out_proj_ln",
    )(yabc, yd, x, wo, g, b)


def _ffn_kernel(alpha, moe, x_ref, wr_ref, wg_ref, wu_ref, wd_ref, g_ref, b_ref, o_ref,
                xb_ref, acc_ref, comb_ref):
    e = pl.program_id(1)
    f = pl.program_id(2)
    last = (e == pl.num_programs(1) - 1) & (f == pl.num_programs(2) - 1)

    @pl.when((e == 0) & (f == 0))
    def _():
        x = x_ref[...]
        xb_ref[...] = x.astype(BF16)
        acc_ref[...] = jnp.zeros(acc_ref.shape, F32)
        if moe:
            logits = jnp.dot(x, wr_ref[...], preferred_element_type=F32, precision=lax.Precision.HIGHEST)
            lane = lax.broadcasted_iota(jnp.int32, logits.shape, 1)
            logits = jnp.where(lane < N_EXPERTS, logits, NEG)
            big = logits.shape[1]
            m1 = jnp.max(logits, axis=1, keepdims=True)
            i1 = jnp.min(jnp.where(logits == m1, lane, big), axis=1, keepdims=True)
            rest = jnp.where(lane == i1, NEG, logits)
            m2 = jnp.max(rest, axis=1, keepdims=True)
            i2 = jnp.min(jnp.where(rest == m2, lane, big), axis=1, keepdims=True)
            e2 = jnp.exp(m2 - m1)
            g1 = 1.0 / (1.0 + e2)
            g2 = e2 / (1.0 + e2)
            comb_ref[...] = jnp.where(lane == i1, g1, 0.0) + jnp.where(lane == i2, g2, 0.0)

    xb = xb_ref[...]
    h = _silu(_dot(xb, wg_ref[0])) * _dot(xb, wu_ref[0])
    if moe:
        lane = lax.broadcasted_iota(jnp.int32, comb_ref.shape, 1)
        h = h * jnp.sum(jnp.where(lane == e, comb_ref[...], 0.0), axis=1, keepdims=True)
    acc_ref[...] += _dot(h.astype(BF16), wd_ref[0])

    @pl.when(last)
    def _():
        o_ref[...] = _layer_norm(alpha * x_ref[...] + acc_ref[...], g_ref[...], b_ref[...])


def _ffn(x, wr, wg, wu, wd, g, b, alpha, moe, tm, tf):
    n = x.shape[0]
    n_e, _, ff = wg.shape
    row = lambda i, e, f: (i, 0)
    full = lambda i, e, f: (0, 0)
    return pl.pallas_call(
        functools.partial(_ffn_kernel, alpha, moe),
        out_shape=jax.ShapeDtypeStruct((n, D_MODEL), F32),
        grid=(n // tm, n_e, ff // tf),
        in_specs=[pl.BlockSpec((tm, D_MODEL), row),
                  pl.BlockSpec(wr.shape, full),
                  pl.BlockSpec((1, D_MODEL, tf), lambda i, e, f: (e, 0, f)),
                  pl.BlockSpec((1, D_MODEL, tf), lambda i, e, f: (e, 0, f)),
                  pl.BlockSpec((1, tf, D_MODEL), lambda i, e, f: (e, f, 0)),
                  pl.BlockSpec((1, D_MODEL), full), pl.BlockSpec((1, D_MODEL), full)],
        out_specs=pl.BlockSpec((tm, D_MODEL), row),
        scratch_shapes=[pltpu.VMEM((tm, D_MODEL), BF16), pltpu.VMEM((tm, D_MODEL), F32),
                        pltpu.VMEM((tm, 128), F32)],
        compiler_params=pltpu.CompilerParams(
            dimension_semantics=("parallel", "arbitrary", "arbitrary"), vmem_limit_bytes=VMEM_LIMIT),
        name="moe_ln" if moe else "ffn_ln",
    )(x, wr, wg, wu, wd, g, b)


def _w_in_columns():
    o = {}
    off = 0
    for name, size in (("a_in", 256), ("a_gb", 256), ("a_gc", 256), ("b_qkv", 768), ("b_a", 4), ("b_b", 4),
                       ("b_z", 256), ("c_q", 256), ("c_f", 256), ("c_i", 256), ("c_g", 256),
                       ("d_q", 256), ("d_k", 256), ("d_v", 256)):
        o[name] = np.arange(off, off + size)
        off += size
    rep = lambda ix: np.repeat(ix, HEAD_DIM)
    cols = np.concatenate([o["a_in"], o["a_gb"], o["a_gc"], o["b_z"],
                           o["b_qkv"], rep(o["b_a"]),
                           o["c_q"], o["c_f"], o["c_i"], o["c_g"],
                           rep(o["b_b"]), o["d_q"], o["d_k"], o["d_v"]])
    assert cols.shape[0] == U_W
    return cols


def _pick_tile(n, pref):
    t = min(pref, n)
    while n % t:
        t //= 2
    return t


def _blockdiag_state(s):
    bsz = s.shape[0]
    eye = jnp.eye(N_HEADS, dtype=s.dtype)
    return (s[:, :, :, None, :] * eye[None, :, None, :, None]).reshape(bsz, GROUP_W, GROUP_W)


def _diag_blocks(s):
    bsz = s.shape[0]
    s5 = s.reshape(bsz, N_HEADS, HEAD_DIM, N_HEADS, HEAD_DIM)
    return jnp.stack([s5[:, h, :, h, :] for h in range(N_HEADS)], axis=1)


def _run_group(x, init, attend, lw, depth):
    bsz, seq, _ = x.shape
    lp = -(-seq // CHUNK) * CHUNK
    assert (lp == seq or seq < CHUNK) and seq >= CONV_B_W - 1
    n = bsz * seq
    alpha = (2 * depth) ** 0.25
    xf = x.reshape(n, D_MODEL)
    tm = _pick_tile(n, 512)
    outs = []
    for l in range(depth):
        w = lw[l]
        u = _proj(xf, w["w_in"], tm, 1024)
        u3 = u.reshape(bsz, seq, U_W)
        if init is None:
            hist_a = jnp.zeros((bsz, HIST, 256), F32)
            hist_b = jnp.zeros((bsz, HIST, 768), F32)
            sg0 = jnp.zeros((bsz, GROUP_W, GROUP_W), F32)
            sh0 = sg0
        else:
            ca, cb, s_g, s_h = init[l]
            hist_a = jnp.pad(ca, ((0, 0), (HIST - ca.shape[1], 0), (0, 0)))
            hist_b = jnp.pad(cb, ((0, 0), (HIST - cb.shape[1], 0), (0, 0)))
            sg0 = _blockdiag_state(s_g)
            sh0 = _blockdiag_state(jnp.swapaxes(s_h, 2, 3))
        u_rec = u if lp == seq else jnp.pad(u3, ((0, 0), (0, lp - seq), (0, 0))).reshape(bsz * lp, U_W)
        yabc, sg1, sh1, tail_a = _recur(u_rec, min(seq, CHUNK) if lp != seq else CHUNK,
                                        hist_a, hist_b, sg0, sh0, w["conv_a"], w["conv_b"], w["prm"])
        if lp != seq:
            yabc = yabc.reshape(bsz, lp, 768)[:, :seq].reshape(n, 768)
        yd = attend(l, u, u3, w)
        x1 = _oproj(yabc, yd, xf, w["w_o"], w["ln1_g"], w["ln1_b"], alpha, tm)
        f = w["ffn"]
        xf = _ffn(x1, f["wr"], f["wg"], f["wu"], f["wd"], w["ln2_g"], w["ln2_b"], alpha, f["moe"],
                  _pick_tile(n, 512), f["tf"])
        a0 = HIST - (CONV_A_W - 1)
        outs.append((
            u3[:, :, C_DK:C_DK + 256].reshape(bsz, seq, N_HEADS, HEAD_DIM),
            u3[:, :, C_DV:C_DV + 256].reshape(bsz, seq, N_HEADS, HEAD_DIM),
            tail_a[:, a0:a0 + CONV_A_W - 1],
            u3[:, seq - (CONV_B_W - 1):, C_BQKV:C_BQKV + 768],
            _diag_blocks(sg1),
            jnp.swapaxes(_diag_blocks(sh1), 2, 3),
        ))
    return xf.reshape(bsz, seq, D_MODEL), [jnp.stack([o[i] for o in outs]) for i in range(6)]


def kernel(x_prompt, x_sample, cache_k, cache_v, state_conv_a, state_conv_b, state_gdn, state_hgrn, page_table, w_in, conv_a, conv_b, gdn_a_log, gdn_dt_bias, norm_b, lower_bounds, norm_c, lambda_q1, lambda_k1, lambda_q2, lambda_k2, norm_d, rel_bias, w_o, ln1_g, ln1_b, ffn_w_gate, ffn_w_up, ffn_w_down, router_w, moe_w_gate, moe_w_up, moe_w_down, ln2_g, ln2_b):
    depth = w_in.shape[0]
    cols = _w_in_columns()
    rep = lambda t: jnp.repeat(t.astype(F32), HEAD_DIM)
    tile4 = lambda t: jnp.tile(t.astype(F32), N_HEADS)
    lbs = jax.nn.softmax(lower_bounds.astype(F32), axis=0)
    lb_all = jnp.cumsum(lbs, axis=0) - lbs[0]
    bd = jnp.asarray(_recur_consts()["bd"])
    lw = []
    for l in range(depth):
        lam_init = 0.8 - 0.6 * math.exp(-0.3 * l)
        lam = (jnp.exp(jnp.sum(lambda_q1[l].astype(F32) * lambda_k1[l].astype(F32)))
               - jnp.exp(jnp.sum(lambda_q2[l].astype(F32) * lambda_k2[l].astype(F32))) + lam_init)
        prm = jnp.stack([-jnp.exp(rep(gdn_a_log[l])), rep(gdn_dt_bias[l]), tile4(norm_b[l]),
                         tile4(norm_c[l]), lb_all[l], jnp.zeros((GROUP_W,), F32),
                         jnp.zeros((GROUP_W,), F32), jnp.zeros((GROUP_W,), F32)])
        j = l // 2
        if l % 2 == 0:
            ff = ffn_w_gate.shape[2]
            ffn = dict(moe=False, wr=jnp.zeros((D_MODEL, 128), F32),
                       wg=ffn_w_gate[j][None].astype(BF16), wu=ffn_w_up[j][None].astype(BF16),
                       wd=ffn_w_down[j][None].astype(BF16))
        else:
            ff = moe_w_gate.shape[3]
            ffn = dict(moe=True, wr=jnp.pad(router_w[j].astype(F32), ((0, 0), (0, 128 - N_EXPERTS))),
                       wg=moe_w_gate[j].astype(BF16), wu=moe_w_up[j].astype(BF16),
                       wd=moe_w_down[j].astype(BF16))
        tf = ff
        for cand in (1408, 1024, 512, 256, 128):
            if ff % cand == 0:
                tf = cand
                break
        ffn["tf"] = tf
        lw.append(dict(
            w_in=w_in[l][:, cols].astype(BF16),
            conv_a=jnp.pad(conv_a[l].astype(F32), ((0, HIST - CONV_A_W), (0, 0))),
            conv_b=jnp.pad(conv_b[l].astype(F32), ((0, HIST - CONV_B_W), (0, 0))),
            prm=prm,
            lam_row=jnp.full((1, GROUP_W), lam, F32),
            nd_row=(tile4(norm_d[l]) * (1.0 - lam_init))[None, :],
            w_o=w_o[l].astype(BF16),
            ln1_g=ln1_g[l][None].astype(F32), ln1_b=ln1_b[l][None].astype(F32),
            ln2_g=ln2_g[l][None].astype(F32), ln2_b=ln2_b[l][None].astype(F32),
            ffn=ffn,
        ))
    relb = rel_bias.astype(F32)

    def attend_prompt(l, u, u3, w):
        bsz, seq, _ = u3.shape
        return _attn_prompt(u, bsz, seq, relb, w["lam_row"], w["nd_row"], bd)

    ck_t = jnp.transpose(cache_k, (0, 1, 3, 4, 2))
    cv_t = jnp.transpose(cache_v, (0, 1, 3, 4, 2))

    def attend_sample(l, u, u3, w):
        bsz, seq, _ = u3.shape

        def heads8(c0):
            t = u3[:, :, c0:c0 + 256].reshape(bsz, seq, N_HEADS, HEAD_DIM)
            return jnp.pad(jnp.swapaxes(t, 1, 2), ((0, 0), (0, 0), (0, 8 - seq), (0, 0)))

        o = _attn_sample(l, heads8(C_DQ), heads8(C_DK), heads8(C_DV), ck_t, cv_t,
                         page_table, seq, relb, w["lam_row"], w["nd_row"])
        return jnp.swapaxes(o[:, :, :seq], 1, 2).reshape(bsz * seq, GROUP_W)

    init_s = [(state_conv_a[l], state_conv_b[l], state_gdn[l], state_hgrn[l]) for l in range(depth)]
    y_p, (k_p, v_p, ca_p, cb_p, sg_p, sh_p) = _run_group(x_prompt, None, attend_prompt, lw, depth)
    y_s, (k_s, v_s, ca_s, cb_s, sg_s, sh_s) = _run_group(x_sample, init_s, attend_sample, lw, depth)
    return (y_p, y_s, k_p, v_p, k_s, v_s, ca_p, ca_s, cb_p, cb_s, sg_p, sg_s, sh_p, sh_s)
```

```python
import functools
import math

import numpy as np
import jax
import jax.numpy as jnp
from jax import lax
from jax.experimental import pallas as pl
from jax.experimental.pallas import tpu as pltpu

F32 = jnp.float32
BF16 = jnp.bfloat16

D_MODEL = 1024
GROUP_W = 256
N_HEADS = 4
HEAD_DIM = 64
CHUNK = 64
HIST = 8
SUB = 16
SUB_SHIFT = 4
CONV_A_W = 3
CONV_B_W = 4
HEAD_DIM_DQK = 32
N_BUCKETS = 32
MAX_EXACT = 16
MAX_DISTANCE = 128
N_EXPERTS = 8
LN_EPS = 1e-5
RMS_EPS = 1e-6
NEG = -1e30
LOG2E = math.log2(math.e)
QK_SCALE_LOG2 = HEAD_DIM_DQK ** -0.5 * LOG2E
U_W = 4096
VMEM_LIMIT = 56 * 1024 * 1024

C_AIN, C_AGB, C_AGC, C_BZ = 0, 256, 512, 768
C_BQKV, C_BA = 1024, 1792
C_CQ, C_CF, C_CI, C_CG = 2048, 2304, 2560, 2816
C_BB, C_DQ, C_DK, C_DV = 3072, 3328, 3584, 3840


def _dot(a, b):
    return jnp.dot(a, b, preferred_element_type=F32)


def _dot_nt(a, b):
    return lax.dot_general(a, b, (((1,), (1,)), ((), ())), preferred_element_type=F32)


def _dot_tn(a, b):
    return lax.dot_general(a, b, (((0,), (0,)), ((), ())), preferred_element_type=F32)


def _split2(x):
    hi = x.astype(BF16)
    lo = (x - hi.astype(F32)).astype(BF16)
    return hi, lo


def _dot_c2(c_bf, x):
    hi, lo = _split2(x)
    return _dot(c_bf, hi) + _dot(c_bf, lo)


def _segsum(s, bd_bf):
    return _dot(s.astype(BF16), bd_bf)


def _sigmoid(x):
    return 0.5 * jnp.tanh(0.5 * x) + 0.5


def _silu(x):
    return x * _sigmoid(x)


def _softplus(x):
    return jnp.maximum(x, 0.0) + jnp.log(1.0 + jnp.exp(-jnp.abs(x)))


def _layer_norm(z, g, b):
    mu = jnp.mean(z, axis=-1, keepdims=True)
    zc = z - mu
    var = jnp.mean(zc * zc, axis=-1, keepdims=True)
    return zc * lax.rsqrt(var + LN_EPS) * g + b


def _proj_kernel(tn, x_ref, w_ref, o_ref):
    xb = x_ref[...].astype(BF16)
    for c0 in range(0, w_ref.shape[1], tn):
        o_ref[:, c0:c0 + tn] = _dot(xb, w_ref[:, c0:c0 + tn])


def _proj(x, w, tm, tn):
    n, k = x.shape
    m = w.shape[1]
    return pl.pallas_call(
        functools.partial(_proj_kernel, tn),
        out_shape=jax.ShapeDtypeStruct((n, m), F32),
        grid=(n // tm,),
        in_specs=[pl.BlockSpec((tm, k), lambda i: (i, 0)),
                  pl.BlockSpec((k, m), lambda i: (0, 0))],
        out_specs=pl.BlockSpec((tm, m), lambda i: (i, 0)),
        compiler_params=pltpu.CompilerParams(
            dimension_semantics=("parallel",), vmem_limit_bytes=VMEM_LIMIT),
        name="proj_in",
    )(x, w)


def _recur_consts():
    c = CHUNK
    w = GROUP_W
    r = np.arange(c)[:, None]
    lane = np.arange(w)[None, :]
    j = lane % c
    bd = (np.arange(w)[:, None] // c == lane // c)
    return dict(
        lincl=(np.arange(c)[None, :] <= r).astype(np.float32),
        ucat=(r <= j).astype(np.float32),
        incl=(j <= r).astype(np.float32),
        strict=(j < r).astype(np.float32),
        eye=(j == r).astype(np.float32),
        bd=bd.astype(np.float32),
    )


def _recur_kernel(l_valid, n_sub, u0_ref, u1_ref, u2_ref, ubb_ref, hista_ref, histb_ref, sg0_ref, sh0_ref,
                  wa_ref, wb_ref, prm_ref, lincl_ref, ucat_ref, incl_ref, strict_ref, eye_ref, bd_ref,
                  y_ref, sg_out_ref, sh_out_ref, hista_out_ref,
                  sg_ref, sh_ref, xpa_ref, xpb_ref, hbp_ref, hkp_ref, gs_ref):
    c = CHUNK
    rows = n_sub * c
    ci = pl.program_id(1)
    nc = pl.num_programs(1)

    @pl.when(ci == 0)
    def _():
        sg_ref[...] = sg0_ref[0]
        sh_ref[...] = sh0_ref[0]
        xpa_ref[0:HIST, :] = hista_ref[0]
        xpb_ref[0:HIST, :] = histb_ref[0]

    bd = bd_ref[...]
    bd_bf = bd.astype(BF16)
    lincl_bf = lincl_ref[...].astype(BF16)
    ones_bf = jnp.ones((c, c), BF16)
    incl = incl_ref[...] > 0.5
    strict = strict_ref[...] > 0.5
    eye = eye_ref[...]
    row = lax.broadcasted_iota(jnp.int32, (c, GROUP_W), 0)
    jcol = lax.broadcasted_iota(jnp.int32, (c, GROUP_W), 1) & (c - 1)
    sub = row >> SUB_SHIFT
    row_in_sub = row & (SUB - 1)
    same_sub = (jcol >> SUB_SHIFT) == sub
    jcol_in_sub = jcol & (SUB - 1)

    def blockdiag(x):
        return jnp.concatenate([x, x, x, x], axis=0) * (bd_bf if x.dtype == BF16 else bd)

    def mmc(x, y):
        return _dot(x.astype(BF16), blockdiag(y.astype(BF16)))

    def mask_rows(x):
        return x if l_valid >= c else jnp.where(row < l_valid, x, 0.0)

    xpa_ref[HIST:HIST + rows, :] = u0_ref[:, C_AGC:C_AGC + 256] * u0_ref[:, C_AIN:C_AIN + 256]
    conv_a = jnp.zeros((rows, GROUP_W), F32)
    for jj in range(CONV_A_W):
        off = HIST - (CONV_A_W - 1) + jj
        conv_a = conv_a + xpa_ref[off:off + rows, :] * wa_ref[jj:jj + 1, :]
    y_ref[:, 0:256] = u0_ref[:, C_AGB:C_AGB + 256] * conv_a
    last = l_valid if n_sub == 1 else rows
    hista_out_ref[0] = xpa_ref[last:last + HIST, :]
    xpa_ref[0:HIST, :] = xpa_ref[rows:rows + HIST, :]
    xpb_ref[HIST:HIST + rows, :] = u1_ref[:, 0:768]

    def chunk(s):
        r0 = s * c
        conv_b = jnp.zeros((c, 768), F32)
        for jj in range(CONV_B_W):
            off = HIST - (CONV_B_W - 1) + jj + r0
            conv_b = conv_b + xpb_ref[off:off + c, :] * wb_ref[jj:jj + 1, :]
        conv_b = _silu(conv_b)
        q_raw = conv_b[:, 0:256]
        k_raw = conv_b[:, 256:512]
        v = conv_b[:, 512:768]
        beta = mask_rows(_sigmoid(ubb_ref[r0:r0 + c, :]))
        g = mask_rows(prm_ref[0:1, :] * _softplus(u1_ref[r0:r0 + c, 768:1024] + prm_ref[1:2, :]))
        lb = prm_ref[4:5, :]
        sig = _sigmoid(u2_ref[r0:r0 + c, 256:512])
        log_f = mask_rows(jnp.log2(lb + (1.0 - lb) * sig))
        k_c = mask_rows((1.0 - lb) * (1.0 - sig))
        q_c = _silu(u2_ref[r0:r0 + c, 0:256])
        c_i = u2_ref[r0:r0 + c, 512:768]
        yield
        q = q_raw * lax.rsqrt(_segsum(q_raw * q_raw, bd_bf) + RMS_EPS) * (HEAD_DIM ** -0.5)
        k = k_raw * lax.rsqrt(_segsum(k_raw * k_raw, bd_bf) + RMS_EPS)
        gcol = _dot_c2(lincl_bf, g)
        grow = _dot_c2(ones_bf, g * ucat_ref[...])
        bcum = _dot_c2(lincl_bf, log_f)
        yield
        decay = jnp.exp2(jnp.where(incl, gcol - grow, NEG))
        k_bf = k.astype(BF16)
        kb_bf = blockdiag(k_bf)
        kk = _dot_nt(k_bf, kb_bf)
        qk = _dot_nt(q.astype(BF16), kb_bf)
        a_mat = jnp.where(strict, kk * decay * beta, 0.0)
        p_mat = qk * decay
        yield
        hbp_ref[s] = bcum
        hkp_ref[s] = k_c
        anchor = jnp.zeros((c, GROUP_W), F32)
        for t in range(1, c // SUB):
            anchor = jnp.where(sub == t, hbp_ref[s, t * SUB - 1:t * SUB, :], anchor)
        q_anch = (q_c * jnp.exp2(bcum - anchor)).astype(BF16)
        att = jnp.zeros((c, GROUP_W), F32)
        for t in range(1, c // SUB):
            a_t = hbp_ref[s, t * SUB - 1:t * SUB, :]
            k_anch = k_c * jnp.exp2(jnp.where(row < t * SUB, a_t - bcum, NEG))
            att = att + jnp.where(sub == t, _dot_nt(q_anch, blockdiag(k_anch.astype(BF16))), 0.0)
        yield
        t_mat = eye - a_mat
        pw = mmc(a_mat, a_mat)
        t_mat = t_mat + mmc(pw, t_mat)
        yield
        def key_rows(ref, jj):
            return jnp.concatenate(
                [jnp.broadcast_to(ref[s, t * SUB + jj:t * SUB + jj + 1, :], (SUB, GROUP_W))
                 for t in range(c // SUB)], axis=0)

        for jj in range(SUB):
            e = jnp.exp2(jnp.where(row_in_sub >= jj, bcum - key_rows(hbp_ref, jj), NEG))
            gs_ref[s, jj * c:(jj + 1) * c, :] = (q_c * e * key_rows(hkp_ref, jj)).astype(BF16)
        attb = _dot(gs_ref[s], bd_bf)
        yield
        pw = mmc(pw, pw)
        t_mat = t_mat + mmc(pw, t_mat)
        yield
        for jj in range(SUB):
            att = att + jnp.where(same_sub & (jcol_in_sub == jj), attb[jj * c:(jj + 1) * c, :], 0.0)
        o_c = _dot(att.astype(BF16), blockdiag(c_i.astype(BF16)))
        yield
        pw = mmc(pw, pw)
        t_mat = t_mat + mmc(pw, t_mat)
        yield
        pw = mmc(pw, pw)
        t_mat = t_mat + mmc(pw, t_mat)
        yield
        pw = mmc(pw, pw)
        t_mat = t_mat + mmc(pw, t_mat)
        egc = jnp.exp2(gcol)
        rhs = jnp.concatenate([blockdiag((v * beta).astype(BF16)),
                               blockdiag((k * beta * egc).astype(BF16))], axis=1)
        sol = _dot(t_mat.astype(BF16), rhs)
        u_sol = sol[:, 0:256]
        w_sol = sol[:, 256:512]
        yield
        sh = sh_ref[...]
        o_c = o_c + _dot_nt((q_c * jnp.exp2(bcum)).astype(BF16), sh.astype(BF16))
        blast = bcum[c - 1:c, :]
        kd_c = k_c * jnp.exp2(blast - bcum)
        sh_ref[...] = sh * jnp.exp2(blast) + _dot_tn(c_i.astype(BF16), kd_c.astype(BF16)) * bd
        ms_c = _segsum(o_c * o_c, bd_bf) * (1.0 / HEAD_DIM)
        y_ref[r0:r0 + c, 512:768] = (o_c * lax.rsqrt(ms_c + RMS_EPS) * prm_ref[3:4, :]
                                     * _silu(u2_ref[r0:r0 + c, 768:1024]))
        yield
        sg = sg_ref[...]
        sg_bf = sg.astype(BF16)
        v_new = u_sol - _dot(w_sol.astype(BF16), sg_bf)
        o_b = _dot((q * egc).astype(BF16), sg_bf) + _dot(p_mat.astype(BF16), blockdiag(v_new.astype(BF16)))
        glast = gcol[c - 1:c, :]
        kd = k * jnp.exp2(glast - gcol)
        sg_ref[...] = sg * jnp.exp2(glast) + _dot_tn(kd.astype(BF16), v_new.astype(BF16)) * bd
        ms_b = _segsum(o_b * o_b, bd_bf) * (1.0 / HEAD_DIM)
        y_ref[r0:r0 + c, 256:512] = (o_b * lax.rsqrt(ms_b + RMS_EPS) * prm_ref[2:3, :]
                                     * _silu(u0_ref[r0:r0 + c, C_BZ:C_BZ + 256]))

    stages = [chunk(s) for s in range(n_sub)]
    live = True
    while live:
        live = False
        for gen in stages:
            if next(gen, "done") != "done":
                live = True
    xpb_ref[0:HIST, :] = xpb_ref[rows:rows + HIST, :]

    @pl.when(ci == nc - 1)
    def _():
        sg_out_ref[0] = sg_ref[...]
        sh_out_ref[0] = sh_ref[...]


def _recur(u, l_valid, hist_a, hist_b, sg0, sh0, wa, wb, prm):
    bsz = sg0.shape[0]
    n = u.shape[0]
    n_sub = 2 if (n // bsz) % (2 * CHUNK) == 0 else 1
    rows = n_sub * CHUNK
    nc = n // bsz // rows
    consts = _recur_consts()
    cnames = ("lincl", "ucat", "incl", "strict", "eye", "bd")
    cvals = [jnp.asarray(consts[k]) for k in cnames]
    row_map = lambda col: (lambda b, ci: (b * nc + ci, col))
    full2 = lambda b, ci: (0, 0)
    per_b = lambda b, ci: (b, 0, 0)
    in_specs = [
        pl.BlockSpec((rows, 1024), row_map(0)),
        pl.BlockSpec((rows, 1024), row_map(1)),
        pl.BlockSpec((rows, 1024), row_map(2)),
        pl.BlockSpec((rows, 256), row_map(C_BB // 256)),
        pl.BlockSpec((1, HIST, 256), per_b),
        pl.BlockSpec((1, HIST, 768), per_b),
        pl.BlockSpec((1, 256, 256), per_b),
        pl.BlockSpec((1, 256, 256), per_b),
        pl.BlockSpec(wa.shape, full2),
        pl.BlockSpec(wb.shape, full2),
        pl.BlockSpec(prm.shape, full2),
    ] + [pl.BlockSpec(cv.shape, full2) for cv in cvals]
    out_shape = (jax.ShapeDtypeStruct((n, 768), F32),
                 jax.ShapeDtypeStruct((bsz, 256, 256), F32),
                 jax.ShapeDtypeStruct((bsz, 256, 256), F32),
                 jax.ShapeDtypeStruct((bsz, HIST, 256), F32))
    out_specs = (pl.BlockSpec((rows, 768), lambda b, ci: (b * nc + ci, 0)),
                 pl.BlockSpec((1, 256, 256), per_b),
                 pl.BlockSpec((1, 256, 256), per_b),
                 pl.BlockSpec((1, HIST, 256), per_b))
    scratch = [pltpu.VMEM((256, 256), F32), pltpu.VMEM((256, 256), F32),
               pltpu.VMEM((rows + HIST, 256), F32), pltpu.VMEM((rows + HIST, 768), F32),
               pltpu.VMEM((n_sub, CHUNK, 256), F32), pltpu.VMEM((n_sub, CHUNK, 256), F32),
               pltpu.VMEM((n_sub, SUB * CHUNK, 256), BF16)]
    assert n_sub == 1 or l_valid == CHUNK
    return pl.pallas_call(
        functools.partial(_recur_kernel, l_valid, n_sub),
        out_shape=out_shape,
        grid=(bsz, nc),
        in_specs=in_specs,
        out_specs=out_specs,
        scratch_shapes=scratch,
        compiler_params=pltpu.CompilerParams(
            dimension_semantics=("parallel", "arbitrary"), vmem_limit_bytes=VMEM_LIMIT),
        name="recur",
    )(u, u, u, u, hist_a, hist_b, sg0, sh0, wa, wb, prm, *cvals)


def _t5_bucket_np(dist):
    n = np.maximum(dist, 0)
    nf = np.maximum(n, MAX_EXACT).astype(np.float32)
    ratio = (np.log(nf / np.float32(MAX_EXACT)) / np.float32(math.log(MAX_DISTANCE / MAX_EXACT))).astype(np.float32)
    large = MAX_EXACT + (ratio * np.float32(N_BUCKETS - MAX_EXACT)).astype(np.int32)
    large = np.minimum(large, N_BUCKETS - 1)
    return np.where(n < MAX_EXACT, n, large).astype(np.int32)


def _bias_from_buckets(bkt, relb_ref, h):
    out = jnp.full(bkt.shape, NEG, F32)
    for b in range(N_BUCKETS):
        out = jnp.where(bkt == b, relb_ref[b, h] * LOG2E, out)
    return out


def _head_lane_mask(h, shape):
    lane = lax.broadcasted_iota(jnp.int32, shape, len(shape) - 1)
    return (lane >= h * HEAD_DIM) & (lane < (h + 1) * HEAD_DIM)


def _map_lane_mask(h, m, shape):
    lane = lax.broadcasted_iota(jnp.int32, shape, len(shape) - 1)
    lo = h * HEAD_DIM + m * HEAD_DIM_DQK
    return (lane >= lo) & (lane < lo + HEAD_DIM_DQK)


def _attn_finish(o1, o2, lam_ref, nd_ref, bd_bf):
    od = o1 - lam_ref[...] * o2
    ms = _segsum(od * od, bd_bf) * (1.0 / HEAD_DIM)
    return od * lax.rsqrt(ms + RMS_EPS) * nd_ref[...]


def _attn_prompt_kernel(qi_ref, kj_ref, relb_ref, q_ref, k_ref, v_ref, bkt_ref, lam_ref, nd_ref, bd_ref, o_ref,
                        q8_ref, m_ref, l_ref, acc_ref, bias_ref, s_ref, p_ref, al_ref):
    b = pl.program_id(0)
    step = pl.program_id(1)
    i = qi_ref[step]
    j = kj_ref[step]

    @pl.when((b == 0) & (step == 0))
    def _():
        for d in range(3):
            bkt = bkt_ref[d]
            for h in range(N_HEADS):
                bias_ref[d, h] = _bias_from_buckets(bkt, relb_ref, h)

    @pl.when(j == 0)
    def _():
        qt = (q_ref[...] * QK_SCALE_LOG2).T
        feat = lax.broadcasted_iota(jnp.int32, qt.shape, 0)
        for hm in range(2 * N_HEADS):
            lo = hm * HEAD_DIM_DQK
            q8_ref[hm] = jnp.where((feat >= lo) & (feat < lo + HEAD_DIM_DQK), qt, 0.0).astype(BF16)
        m_ref[...] = jnp.full(m_ref.shape, NEG, F32)
        l_ref[...] = jnp.zeros(l_ref.shape, F32)
        acc_ref[...] = jnp.zeros(acc_ref.shape, F32)

    def update():
        kb = k_ref[...].astype(BF16)
        vt = v_ref[...].T.astype(BF16)
        d = jnp.minimum(i - j, 2)
        for hm in range(2 * N_HEADS):
            s_ref[hm] = _dot(kb, q8_ref[hm]) + bias_ref[d, hm // 2]
        for hm in range(2 * N_HEADS):
            s = s_ref[hm]
            m_old = m_ref[hm]
            m_new = jnp.maximum(m_old, jnp.max(s, axis=0, keepdims=True))
            p = jnp.exp2(s - m_new)
            alpha = jnp.exp2(m_old - m_new)
            l_ref[hm] = alpha * l_ref[hm] + jnp.sum(p, axis=0, keepdims=True)
            p_ref[hm] = p.astype(BF16)
            al_ref[hm] = alpha
            m_ref[hm] = m_new
        for hm in range(2 * N_HEADS):
            h = hm // 2
            acc_ref[hm] = al_ref[hm] * acc_ref[hm] + _dot(vt[h * HEAD_DIM:(h + 1) * HEAD_DIM, :], p_ref[hm])

    update()

    @pl.when(j == i)
    def _():
        o1 = jnp.concatenate([acc_ref[2 * h] / l_ref[2 * h] for h in range(N_HEADS)], axis=0)
        o2 = jnp.concatenate([acc_ref[2 * h + 1] / l_ref[2 * h + 1] for h in range(N_HEADS)], axis=0)
        o_ref[...] = _attn_finish(o1.T, o2.T, lam_ref, nd_ref, bd_ref[...].astype(BF16))


def _attn_prompt(u, bsz, seq, rel_bias, lam_row, nd_row, bd):
    t = min(256, seq)
    nq = seq // t
    dist = [d * t + np.arange(t)[None, :] - np.arange(t)[:, None] for d in range(3)]
    bkt = np.stack([np.where(dd >= 0, _t5_bucket_np(dd), -1) for dd in dist]).astype(np.int32)
    assert nq <= 2 or t >= MAX_DISTANCE, "blocks two or more tiles back must share the last bucket"
    pairs = [(i, j) for i in range(nq) for j in range(i + 1)]
    qi = jnp.asarray(np.array([p[0] for p in pairs], np.int32))
    kj = jnp.asarray(np.array([p[1] for p in pairs], np.int32))
    full2 = lambda b, s, qi, kj: (0, 0)
    grid_spec = pltpu.PrefetchScalarGridSpec(
        num_scalar_prefetch=2,
        grid=(bsz, len(pairs)),
        in_specs=[
            pl.BlockSpec(memory_space=pltpu.SMEM),
            pl.BlockSpec((t, GROUP_W), lambda b, s, qi, kj: (b * nq + qi[s], C_DQ // 256)),
            pl.BlockSpec((t, GROUP_W), lambda b, s, qi, kj: (b * nq + kj[s], C_DK // 256)),
            pl.BlockSpec((t, GROUP_W), lambda b, s, qi, kj: (b * nq + kj[s], C_DV // 256)),
            pl.BlockSpec((3, t, t), lambda b, s, qi, kj: (0, 0, 0)),
            pl.BlockSpec((1, GROUP_W), full2),
            pl.BlockSpec((1, GROUP_W), full2),
            pl.BlockSpec((GROUP_W, GROUP_W), full2),
        ],
        out_specs=pl.BlockSpec((t, GROUP_W), lambda b, s, qi, kj: (b * nq + qi[s], 0)),
        scratch_shapes=[pltpu.VMEM((8, GROUP_W, t), BF16), pltpu.VMEM((8, 1, t), F32),
                        pltpu.VMEM((8, 1, t), F32), pltpu.VMEM((8, HEAD_DIM, t), F32),
                        pltpu.VMEM((3, N_HEADS, t, t), F32), pltpu.VMEM((8, t, t), F32),
                        pltpu.VMEM((8, t, t), BF16), pltpu.VMEM((8, 1, t), F32)],
    )
    return pl.pallas_call(
        _attn_prompt_kernel,
        out_shape=jax.ShapeDtypeStruct((bsz * seq, GROUP_W), F32),
        grid_spec=grid_spec,
        compiler_params=pltpu.CompilerParams(
            dimension_semantics=("arbitrary", "arbitrary"), vmem_limit_bytes=VMEM_LIMIT),
        name="attn_prompt",
    )(qi, kj, rel_bias, u, u, u, jnp.asarray(bkt), lam_row, nd_row, bd)


def _attn_sample_kernel(pp, n_steps, pt_ref, relb_ref, q_ref, kn_ref, vn_ref, *rest):
    k_refs = rest[:pp]
    v_refs = rest[pp:2 * pp]
    (bktl_ref, bktn_ref, lam_ref, nd_ref, o_ref,
     q16_ref, m_ref, l_ref, acc_ref, bias_ref, biasn_ref) = rest[2 * pp:]
    b = pl.program_id(0)
    s_id = pl.program_id(1)
    rows = 8
    page = k_refs[0].shape[-1]

    @pl.when((b == 0) & (s_id == 0))
    def _():
        for h in range(N_HEADS):
            near = _bias_from_buckets(bktl_ref[...], relb_ref, h)
            far = jnp.full((rows, page), relb_ref[N_BUCKETS - 1, h] * LOG2E, F32)
            newb = _bias_from_buckets(bktn_ref[...], relb_ref, h)
            for m in range(2):
                r0 = (2 * h + m) * rows
                for t in range(pp):
                    bias_ref[0, r0:r0 + rows, t * page:(t + 1) * page] = far
                    bias_ref[1, r0:r0 + rows, t * page:(t + 1) * page] = near if t == pp - 1 else far
                biasn_ref[r0:r0 + rows, :] = newb

    hrows = 2 * rows

    @pl.when(s_id == 0)
    def _():
        lane = lax.broadcasted_iota(jnp.int32, (rows, HEAD_DIM), 1)
        for h in range(N_HEADS):
            qh = q_ref[0, h] * QK_SCALE_LOG2
            q16_ref[h * hrows:h * hrows + rows, :] = jnp.where(lane < HEAD_DIM_DQK, qh, 0.0)
            q16_ref[h * hrows + rows:(h + 1) * hrows, :] = jnp.where(lane >= HEAD_DIM_DQK, qh, 0.0)
        s = jnp.concatenate(
            [_dot_nt(q16_ref[h * hrows:(h + 1) * hrows, :].astype(BF16), kn_ref[0, h].astype(BF16))
             for h in range(N_HEADS)], axis=0) + biasn_ref[...]
        m0 = jnp.max(s, axis=1, keepdims=True)
        p = jnp.exp2(s - m0)
        m_ref[...] = m0
        l_ref[...] = jnp.sum(p, axis=1, keepdims=True)
        for h in range(N_HEADS):
            acc_ref[h * hrows:(h + 1) * hrows, :] = _dot(
                p[h * hrows:(h + 1) * hrows, :].astype(BF16), vn_ref[0, h].astype(BF16))

    sel = jnp.where(s_id == n_steps - 1, 1, 0)
    s = jnp.concatenate(
        [_dot(q16_ref[h * hrows:(h + 1) * hrows, :].astype(BF16),
              jnp.concatenate([r[0, 0, h] for r in k_refs], axis=1).astype(BF16))
         for h in range(N_HEADS)], axis=0) + bias_ref[sel]
    m_old = m_ref[...]
    m_new = jnp.maximum(m_old, jnp.max(s, axis=1, keepdims=True))
    p32 = jnp.exp2(s - m_new)
    alpha = jnp.exp2(m_old - m_new)
    l_ref[...] = alpha * l_ref[...] + jnp.sum(p32, axis=1, keepdims=True)
    p = p32.astype(BF16)
    pv = jnp.concatenate(
        [_dot_nt(p[h * hrows:(h + 1) * hrows, :],
                 jnp.concatenate([r[0, 0, h] for r in v_refs], axis=1).astype(BF16))
         for h in range(N_HEADS)], axis=0)
    acc_ref[...] = alpha * acc_ref[...] + pv
    m_ref[...] = m_new

    @pl.when(s_id == n_steps - 1)
    def _():
        o = acc_ref[...] / l_ref[...]
        for h in range(N_HEADS):
            od = o[h * hrows:h * hrows + rows, :] - lam_ref[...] * o[h * hrows + rows:(h + 1) * hrows, :]
            ms = jnp.mean(od * od, axis=1, keepdims=True)
            o_ref[0, h] = od * lax.rsqrt(ms + RMS_EPS) * nd_ref[...]


def _attn_sample(layer, q4, k4, v4, ck, cv, page_table, l_new, rel_bias, lam_row, nd_row):
    bsz = q4.shape[0]
    n_pages = page_table.shape[1]
    page = ck.shape[-1]
    pp = 16
    while n_pages % pp:
        pp //= 2
    n_steps = n_pages // pp
    past = n_pages * page
    qpos = past + np.arange(8)[:, None]
    bkt_last = _t5_bucket_np(qpos - (past - page + np.arange(page))[None, :])
    dist_new = np.arange(8)[:, None] - np.arange(8)[None, :]
    ok = (dist_new >= 0) & (np.arange(8)[None, :] < l_new)
    bkt_new = np.where(ok, _t5_bucket_np(dist_new), -1).astype(np.int32)
    assert page >= MAX_DISTANCE, "only the last cache page may need distance-dependent bias"

    def page_map(t):
        return lambda b, s, pt: (layer, pt[b, s * pp + t], 0, 0, 0)

    per_b = lambda b, s, pt: (b, 0, 0, 0)
    full2 = lambda b, s, pt: (0, 0)
    new_spec = pl.BlockSpec((1, N_HEADS, 8, HEAD_DIM), per_b)
    page_spec = lambda t: pl.BlockSpec((1, 1, N_HEADS, HEAD_DIM, page), page_map(t))
    in_specs = ([pl.BlockSpec(memory_space=pltpu.SMEM), new_spec, new_spec, new_spec]
                + [page_spec(t) for t in range(pp)]
                + [page_spec(t) for t in range(pp)]
                + [pl.BlockSpec((8, page), full2), pl.BlockSpec((8, 8), full2),
                   pl.BlockSpec((1, HEAD_DIM), full2), pl.BlockSpec((1, HEAD_DIM), full2)])
    grid_spec = pltpu.PrefetchScalarGridSpec(
        num_scalar_prefetch=1,
        grid=(bsz, n_steps),
        in_specs=in_specs,
        out_specs=new_spec,
        scratch_shapes=[pltpu.VMEM((N_HEADS * 16, HEAD_DIM), F32), pltpu.VMEM((N_HEADS * 16, 1), F32),
                        pltpu.VMEM((N_HEADS * 16, 1), F32), pltpu.VMEM((N_HEADS * 16, HEAD_DIM), F32),
                        pltpu.VMEM((2, N_HEADS * 16, pp * page), F32), pltpu.VMEM((N_HEADS * 16, 8), F32)],
    )
    return pl.pallas_call(
        functools.partial(_attn_sample_kernel, pp, n_steps),
        out_shape=jax.ShapeDtypeStruct((bsz, N_HEADS, 8, HEAD_DIM), F32),
        grid_spec=grid_spec,
        compiler_params=pltpu.CompilerParams(
            dimension_semantics=("arbitrary", "arbitrary"), vmem_limit_bytes=VMEM_LIMIT),
        name="attn_sample",
    )(page_table, rel_bias, q4, k4, v4, *([ck] * pp), *([cv] * pp),
      jnp.asarray(bkt_last), jnp.asarray(bkt_new), lam_row[:, :HEAD_DIM], nd_row[:, :HEAD_DIM])


def _oproj_kernel(alpha, yabc_ref, yd_ref, x_ref, wo_ref, g_ref, b_ref, o_ref):
    mix = (_dot(yabc_ref[...].astype(BF16), wo_ref[0:768, :])
           + _dot(yd_ref[...].astype(BF16), wo_ref[768:1024, :]))
    o_ref[...] = _layer_norm(alpha * x_ref[...] + mix, g_ref[...], b_ref[...])


def _oproj(yabc, yd, x, wo, g, b, alpha, tm):
    n = x.shape[0]
    row = lambda i: (i, 0)
    full = lambda i: (0, 0)
    return pl.pallas_call(
        functools.partial(_oproj_kernel, alpha),
        out_shape=jax.ShapeDtypeStruct((n, D_MODEL), F32),
        grid=(n // tm,),
        in_specs=[pl.BlockSpec((tm, 768), row), pl.BlockSpec((tm, 256), row),
                  pl.BlockSpec((tm, D_MODEL), row), pl.BlockSpec((D_MODEL, D_MODEL), full),
                  pl.BlockSpec((1, D_MODEL), full), pl.BlockSpec((1, D_MODEL), full)],
        out_specs=pl.BlockSpec((tm, D_MODEL), row),
        compiler_params=pltpu.CompilerParams(
            dimension_semantics=("parallel",), vmem_limit_bytes=VMEM_LIMIT),
        name="out_proj_ln",
    )(yabc, yd, x, wo, g, b)


def _ffn_kernel(alpha, moe, x_ref, wr_ref, wg_ref, wu_ref, wd_ref, g_ref, b_ref, o_ref,
                xb_ref, acc_ref, comb_ref):
    e = pl.program_id(1)
    f = pl.program_id(2)
    last = (e == pl.num_programs(1) - 1) & (f == pl.num_programs(2) - 1)

    @pl.when((e == 0) & (f == 0))
    def _():
        x = x_ref[...]
        xb_ref[...] = x.astype(BF16)
        acc_ref[...] = jnp.zeros(acc_ref.shape, F32)
        if moe:
            logits = jnp.dot(x, wr_ref[...], preferred_element_type=F32, precision=lax.Precision.HIGHEST)
            lane = lax.broadcasted_iota(jnp.int32, logits.shape, 1)
            logits = jnp.where(lane < N_EXPERTS, logits, NEG)
            big = logits.shape[1]
            m1 = jnp.max(logits, axis=1, keepdims=True)
            i1 = jnp.min(jnp.where(logits == m1, lane, big), axis=1, keepdims=True)
            rest = jnp.where(lane == i1, NEG, logits)
            m2 = jnp.max(rest, axis=1, keepdims=True)
            i2 = jnp.min(jnp.where(rest == m2, lane, big), axis=1, keepdims=True)
            e2 = jnp.exp(m2 - m1)
            g1 = 1.0 / (1.0 + e2)
            g2 = e2 / (1.0 + e2)
            comb_ref[...] = jnp.where(lane == i1, g1, 0.0) + jnp.where(lane == i2, g2, 0.0)

    xb = xb_ref[...]
    h = _silu(_dot(xb, wg_ref[0])) * _dot(xb, wu_ref[0])
    if moe:
        lane = lax.broadcasted_iota(jnp.int32, comb_ref.shape, 1)
        h = h * jnp.sum(jnp.where(lane == e, comb_ref[...], 0.0), axis=1, keepdims=True)
    acc_ref[...] += _dot(h.astype(BF16), wd_ref[0])

    @pl.when(last)
    def _():
        o_ref[...] = _layer_norm(alpha * x_ref[...] + acc_ref[...], g_ref[...], b_ref[...])


def _ffn(x, wr, wg, wu, wd, g, b, alpha, moe, tm, tf):
    n = x.shape[0]
    n_e, _, ff = wg.shape
    row = lambda i, e, f: (i, 0)
    full = lambda i, e, f: (0, 0)
    return pl.pallas_call(
        functools.partial(_ffn_kernel, alpha, moe),
        out_shape=jax.ShapeDtypeStruct((n, D_MODEL), F32),
        grid=(n // tm, n_e, ff // tf),
        in_specs=[pl.BlockSpec((tm, D_MODEL), row),
                  pl.BlockSpec(wr.shape, full),
                  pl.BlockSpec((1, D_MODEL, tf), lambda i, e, f: (e, 0, f)),
                  pl.BlockSpec((1, D_MODEL, tf), lambda i, e, f: (e, 0, f)),
                  pl.BlockSpec((1, tf, D_MODEL), lambda i, e, f: (e, f, 0)),
                  pl.BlockSpec((1, D_MODEL), full), pl.BlockSpec((1, D_MODEL), full)],
        out_specs=pl.BlockSpec((tm, D_MODEL), row),
        scratch_shapes=[pltpu.VMEM((tm, D_MODEL), BF16), pltpu.VMEM((tm, D_MODEL), F32),
                        pltpu.VMEM((tm, 128), F32)],
        compiler_params=pltpu.CompilerParams(
            dimension_semantics=("parallel", "arbitrary", "arbitrary"), vmem_limit_bytes=VMEM_LIMIT),
        name="moe_ln" if moe else "ffn_ln",
    )(x, wr, wg, wu, wd, g, b)


def _w_in_columns():
    o = {}
    off = 0
    for name, size in (("a_in", 256), ("a_gb", 256), ("a_gc", 256), ("b_qkv", 768), ("b_a", 4), ("b_b", 4),
                       ("b_z", 256), ("c_q", 256), ("c_f", 256), ("c_i", 256), ("c_g", 256),
                       ("d_q", 256), ("d_k", 256), ("d_v", 256)):
        o[name] = np.arange(off, off + size)
        off += size
    rep = lambda ix: np.repeat(ix, HEAD_DIM)
    cols = np.concatenate([o["a_in"], o["a_gb"], o["a_gc"], o["b_z"],
                           o["b_qkv"], rep(o["b_a"]),
                           o["c_q"], o["c_f"], o["c_i"], o["c_g"],
                           rep(o["b_b"]), o["d_q"], o["d_k"], o["d_v"]])
    assert cols.shape[0] == U_W
    return cols


def _pick_tile(n, pref):
    t = min(pref, n)
    while n % t:
        t //= 2
    return t


def _blockdiag_state(s):
    bsz = s.shape[0]
    eye = jnp.eye(N_HEADS, dtype=s.dtype)
    return (s[:, :, :, None, :] * eye[None, :, None, :, None]).reshape(bsz, GROUP_W, GROUP_W)


def _diag_blocks(s):
    bsz = s.shape[0]
    s5 = s.reshape(bsz, N_HEADS, HEAD_DIM, N_HEADS, HEAD_DIM)
    return jnp.stack([s5[:, h, :, h, :] for h in range(N_HEADS)], axis=1)


def _run_group(x, init, attend, lw, depth):
    bsz, seq, _ = x.shape
    lp = -(-seq // CHUNK) * CHUNK
    assert (lp == seq or seq < CHUNK) and seq >= CONV_B_W - 1
    n = bsz * seq
    alpha = (2 * depth) ** 0.25
    xf = x.reshape(n, D_MODEL)
    tm = _pick_tile(n, 512)
    outs = []
    for l in range(depth):
        w = lw[l]
        u = _proj(xf, w["w_in"], tm, 1024)
        u3 = u.reshape(bsz, seq, U_W)
        if init is None:
            hist_a = jnp.zeros((bsz, HIST, 256), F32)
            hist_b = jnp.zeros((bsz, HIST, 768), F32)
            sg0 = jnp.zeros((bsz, GROUP_W, GROUP_W), F32)
            sh0 = sg0
        else:
            ca, cb, s_g, s_h = init[l]
            hist_a = jnp.pad(ca, ((0, 0), (HIST - ca.shape[1], 0), (0, 0)))
            hist_b = jnp.pad(cb, ((0, 0), (HIST - cb.shape[1], 0), (0, 0)))
            sg0 = _blockdiag_state(s_g)
            sh0 = _blockdiag_state(jnp.swapaxes(s_h, 2, 3))
        u_rec = u if lp == seq else jnp.pad(u3, ((0, 0), (0, lp - seq), (0, 0))).reshape(bsz * lp, U_W)
        yabc, sg1, sh1, tail_a = _recur(u_rec, min(seq, CHUNK) if lp != seq else CHUNK,
                                        hist_a, hist_b, sg0, sh0, w["conv_a"], w["conv_b"], w["prm"])
        if lp != seq:
            yabc = yabc.reshape(bsz, lp, 768)[:, :seq].reshape(n, 768)
        yd = attend(l, u, u3, w)
        x1 = _oproj(yabc, yd, xf, w["w_o"], w["ln1_g"], w["ln1_b"], alpha, tm)
        f = w["ffn"]
        xf = _ffn(x1, f["wr"], f["wg"], f["wu"], f["wd"], w["ln2_g"], w["ln2_b"], alpha, f["moe"],
                  _pick_tile(n, 512), f["tf"])
        a0 = HIST - (CONV_A_W - 1)
        outs.append((
            u3[:, :, C_DK:C_DK + 256].reshape(bsz, seq, N_HEADS, HEAD_DIM),
            u3[:, :, C_DV:C_DV + 256].reshape(bsz, seq, N_HEADS, HEAD_DIM),
            tail_a[:, a0:a0 + CONV_A_W - 1],
            u3[:, seq - (CONV_B_W - 1):, C_BQKV:C_BQKV + 768],
            _diag_blocks(sg1),
            jnp.swapaxes(_diag_blocks(sh1), 2, 3),
        ))
    return xf.reshape(bsz, seq, D_MODEL), [jnp.stack([o[i] for o in outs]) for i in range(6)]


def kernel(x_prompt, x_sample, cache_k, cache_v, state_conv_a, state_conv_b, state_gdn, state_hgrn, page_table, w_in, conv_a, conv_b, gdn_a_log, gdn_dt_bias, norm_b, lower_bounds, norm_c, lambda_q1, lambda_k1, lambda_q2, lambda_k2, norm_d, rel_bias, w_o, ln1_g, ln1_b, ffn_w_gate, ffn_w_up, ffn_w_down, router_w, moe_w_gate, moe_w_up, moe_w_down, ln2_g, ln2_b):
    depth = w_in.shape[0]
    cols = _w_in_columns()
    rep = lambda t: jnp.repeat(t.astype(F32), HEAD_DIM)
    tile4 = lambda t: jnp.tile(t.astype(F32), N_HEADS)
    lbs = jax.nn.softmax(lower_bounds.astype(F32), axis=0)
    lb_all = jnp.cumsum(lbs, axis=0) - lbs[0]
    bd = jnp.asarray(_recur_consts()["bd"])
    lw = []
    for l in range(depth):
        lam_init = 0.8 - 0.6 * math.exp(-0.3 * l)
        lam = (jnp.exp(jnp.sum(lambda_q1[l].astype(F32) * lambda_k1[l].astype(F32)))
               - jnp.exp(jnp.sum(lambda_q2[l].astype(F32) * lambda_k2[l].astype(F32))) + lam_init)
        prm = jnp.stack([-jnp.exp(rep(gdn_a_log[l])) * LOG2E, rep(gdn_dt_bias[l]), tile4(norm_b[l]),
                         tile4(norm_c[l]), lb_all[l], jnp.zeros((GROUP_W,), F32),
                         jnp.zeros((GROUP_W,), F32), jnp.zeros((GROUP_W,), F32)])
        j = l // 2
        if l % 2 == 0:
            ff = ffn_w_gate.shape[2]
            ffn = dict(moe=False, wr=jnp.zeros((D_MODEL, 128), F32),
                       wg=ffn_w_gate[j][None].astype(BF16), wu=ffn_w_up[j][None].astype(BF16),
                       wd=ffn_w_down[j][None].astype(BF16))
        else:
            ff = moe_w_gate.shape[3]
            ffn = dict(moe=True, wr=jnp.pad(router_w[j].astype(F32), ((0, 0), (0, 128 - N_EXPERTS))),
                       wg=moe_w_gate[j].astype(BF16), wu=moe_w_up[j].astype(BF16),
                       wd=moe_w_down[j].astype(BF16))
        tf = ff
        for cand in (1408, 1024, 512, 256, 128):
            if ff % cand == 0:
                tf = cand
                break
        ffn["tf"] = tf
        lw.append(dict(
            w_in=w_in[l][:, cols].astype(BF16),
            conv_a=jnp.pad(conv_a[l].astype(F32), ((0, HIST - CONV_A_W), (0, 0))),
            conv_b=jnp.pad(conv_b[l].astype(F32), ((0, HIST - CONV_B_W), (0, 0))),
            prm=prm,
            lam_row=jnp.full((1, GROUP_W), lam, F32),
            nd_row=(tile4(norm_d[l]) * (1.0 - lam_init))[None, :],
            w_o=w_o[l].astype(BF16),
            ln1_g=ln1_g[l][None].astype(F32), ln1_b=ln1_b[l][None].astype(F32),
            ln2_g=ln2_g[l][None].astype(F32), ln2_b=ln2_b[l][None].astype(F32),
            ffn=ffn,
        ))
    relb = rel_bias.astype(F32)

    def attend_prompt(l, u, u3, w):
        bsz, seq, _ = u3.shape
        return _attn_prompt(u, bsz, seq, relb, w["lam_row"], w["nd_row"], bd)

    ck_t = jnp.transpose(cache_k, (0, 1, 3, 4, 2))
    cv_t = jnp.transpose(cache_v, (0, 1, 3, 4, 2))

    def attend_sample(l, u, u3, w):
        bsz, seq, _ = u3.shape

        def heads8(c0):
            t = u3[:, :, c0:c0 + 256].reshape(bsz, seq, N_HEADS, HEAD_DIM)
            return jnp.pad(jnp.swapaxes(t, 1, 2), ((0, 0), (0, 0), (0, 8 - seq), (0, 0)))

        o = _attn_sample(l, heads8(C_DQ), heads8(C_DK), heads8(C_DV), ck_t, cv_t,
                         page_table, seq, relb, w["lam_row"], w["nd_row"])
        return jnp.swapaxes(o[:, :, :seq], 1, 2).reshape(bsz * seq, GROUP_W)

    init_s = [(state_conv_a[l], state_conv_b[l], state_gdn[l], state_hgrn[l]) for l in range(depth)]
    y_p, (k_p, v_p, ca_p, cb_p, sg_p, sh_p) = _run_group(x_prompt, None, attend_prompt, lw, depth)
    y_s, (k_s, v_s, ca_s, cb_s, sg_s, sh_s) = _run_group(x_sample, init_s, attend_sample, lw, depth)
    return (y_p, y_s, k_p, v_p, k_s, v_s, ca_p, ca_s, cb_p, cb_s, sg_p, sg_s, sh_p, sh_s)
```

```python
import functools
import math

import numpy as np
import jax
import jax.numpy as jnp
from jax import lax
from jax.experimental import pallas as pl
from jax.experimental.pallas import tpu as pltpu

F32 = jnp.float32
BF16 = jnp.bfloat16

D_MODEL = 1024
GROUP_W = 256
N_HEADS = 4
HEAD_DIM = 64
CHUNK = 64
HIST = 8
SUB = 16
SUB_SHIFT = 4
CONV_A_W = 3
CONV_B_W = 4
HEAD_DIM_DQK = 32
N_BUCKETS = 32
MAX_EXACT = 16
MAX_DISTANCE = 128
N_EXPERTS = 8
LN_EPS = 1e-5
RMS_EPS = 1e-6
NEG = -1e30
LOG2E = math.log2(math.e)
QK_SCALE_LOG2 = HEAD_DIM_DQK ** -0.5 * LOG2E
U_W = 4096
VMEM_LIMIT = 56 * 1024 * 1024

C_AIN, C_AGB, C_AGC, C_BZ = 0, 256, 512, 768
C_BQKV, C_BA = 1024, 1792
C_CQ, C_CF, C_CI, C_CG = 2048, 2304, 2560, 2816
C_BB, C_DQ, C_DK, C_DV = 3072, 3328, 3584, 3840


def _dot(a, b):
    return jnp.dot(a, b, preferred_element_type=F32)


def _dot_nt(a, b):
    return lax.dot_general(a, b, (((1,), (1,)), ((), ())), preferred_element_type=F32)


def _dot_tn(a, b):
    return lax.dot_general(a, b, (((0,), (0,)), ((), ())), preferred_element_type=F32)


def _split2(x):
    hi = x.astype(BF16)
    lo = (x - hi.astype(F32)).astype(BF16)
    return hi, lo


def _dot_c2(c_bf, x):
    hi, lo = _split2(x)
    return _dot(c_bf, hi) + _dot(c_bf, lo)


def _segsum(s, bd_bf):
    return _dot(s.astype(BF16), bd_bf)


def _sigmoid(x):
    return 0.5 * jnp.tanh(0.5 * x) + 0.5


def _silu(x):
    return x * _sigmoid(x)


def _softplus(x):
    return jnp.maximum(x, 0.0) + jnp.log(1.0 + jnp.exp(-jnp.abs(x)))


def _layer_norm(z, g, b):
    mu = jnp.mean(z, axis=-1, keepdims=True)
    zc = z - mu
    var = jnp.mean(zc * zc, axis=-1, keepdims=True)
    return zc * lax.rsqrt(var + LN_EPS) * g + b


def _proj_kernel(tn, x_ref, w_ref, o_ref):
    xb = x_ref[...].astype(BF16)
    for c0 in range(0, w_ref.shape[1], tn):
        o_ref[:, c0:c0 + tn] = _dot(xb, w_ref[:, c0:c0 + tn])


def _proj(x, w, tm, tn):
    n, k = x.shape
    m = w.shape[1]
    return pl.pallas_call(
        functools.partial(_proj_kernel, tn),
        out_shape=jax.ShapeDtypeStruct((n, m), F32),
        grid=(n // tm,),
        in_specs=[pl.BlockSpec((tm, k), lambda i: (i, 0)),
                  pl.BlockSpec((k, m), lambda i: (0, 0))],
        out_specs=pl.BlockSpec((tm, m), lambda i: (i, 0)),
        compiler_params=pltpu.CompilerParams(
            dimension_semantics=("parallel",), vmem_limit_bytes=VMEM_LIMIT),
        name="proj_in",
    )(x, w)


def _recur_consts():
    c = CHUNK
    w = GROUP_W
    r = np.arange(c)[:, None]
    lane = np.arange(w)[None, :]
    j = lane % c
    bd = (np.arange(w)[:, None] // c == lane // c)
    return dict(
        lincl=(np.arange(c)[None, :] <= r).astype(np.float32),
        ucat=(r <= j).astype(np.float32),
        incl=(j <= r).astype(np.float32),
        strict=(j < r).astype(np.float32),
        eye=(j == r).astype(np.float32),
        bd=bd.astype(np.float32),
    )


def _recur_kernel(l_valid, n_sub, u0_ref, u1_ref, u2_ref, ubb_ref, hista_ref, histb_ref, sg0_ref, sh0_ref,
                  wa_ref, wb_ref, prm_ref, lincl_ref, ucat_ref, incl_ref, strict_ref, eye_ref, bd_ref,
                  y_ref, sg_out_ref, sh_out_ref, hista_out_ref,
                  sg_ref, sh_ref, xpa_ref, xpb_ref, hbp_ref, hkp_ref, gs_ref):
    c = CHUNK
    rows = n_sub * c
    ci = pl.program_id(1)
    nc = pl.num_programs(1)

    @pl.when(ci == 0)
    def _():
        sg_ref[...] = sg0_ref[0]
        sh_ref[...] = sh0_ref[0]
        xpa_ref[0:HIST, :] = hista_ref[0]
        xpb_ref[0:HIST, :] = histb_ref[0]

    bd = bd_ref[...]
    bd_bf = bd.astype(BF16)
    lincl_bf = lincl_ref[...].astype(BF16)
    ones_bf = jnp.ones((c, c), BF16)
    incl = incl_ref[...] > 0.5
    strict = strict_ref[...] > 0.5
    eye = eye_ref[...]
    row = lax.broadcasted_iota(jnp.int32, (c, GROUP_W), 0)
    jcol = lax.broadcasted_iota(jnp.int32, (c, GROUP_W), 1) & (c - 1)
    sub = row >> SUB_SHIFT
    row_in_sub = row & (SUB - 1)
    same_sub = (jcol >> SUB_SHIFT) == sub
    jcol_in_sub = jcol & (SUB - 1)

    def blockdiag(x):
        return jnp.concatenate([x, x, x, x], axis=0) * (bd_bf if x.dtype == BF16 else bd)

    def mmc(x, y):
        return _dot(x.astype(BF16), blockdiag(y.astype(BF16)))

    def mask_rows(x):
        return x if l_valid >= c else jnp.where(row < l_valid, x, 0.0)

    xpa_ref[HIST:HIST + rows, :] = u0_ref[:, C_AGC:C_AGC + 256] * u0_ref[:, C_AIN:C_AIN + 256]
    conv_a = jnp.zeros((rows, GROUP_W), F32)
    for jj in range(CONV_A_W):
        off = HIST - (CONV_A_W - 1) + jj
        conv_a = conv_a + xpa_ref[off:off + rows, :] * wa_ref[jj:jj + 1, :]
    y_ref[:, 0:256] = u0_ref[:, C_AGB:C_AGB + 256] * conv_a
    last = l_valid if n_sub == 1 else rows
    hista_out_ref[0] = xpa_ref[last:last + HIST, :]
    xpa_ref[0:HIST, :] = xpa_ref[rows:rows + HIST, :]
    xpb_ref[HIST:HIST + rows, :] = u1_ref[:, 0:768]

    def chunk(s):
        r0 = s * c
        conv_b = jnp.zeros((c, 768), F32)
        for jj in range(CONV_B_W):
            off = HIST - (CONV_B_W - 1) + jj + r0
            conv_b = conv_b + xpb_ref[off:off + c, :] * wb_ref[jj:jj + 1, :]
        conv_b = _silu(conv_b)
        q_raw = conv_b[:, 0:256]
        k_raw = conv_b[:, 256:512]
        v = conv_b[:, 512:768]
        beta = mask_rows(_sigmoid(ubb_ref[r0:r0 + c, :]))
        g = mask_rows(prm_ref[0:1, :] * _softplus(u1_ref[r0:r0 + c, 768:1024] + prm_ref[1:2, :]))
        lb = prm_ref[4:5, :]
        sig = _sigmoid(u2_ref[r0:r0 + c, 256:512])
        log_f = mask_rows(jnp.log2(lb + (1.0 - lb) * sig))
        k_c = mask_rows((1.0 - lb) * (1.0 - sig))
        q_c = _silu(u2_ref[r0:r0 + c, 0:256])
        c_i = u2_ref[r0:r0 + c, 512:768]
        yield
        q = q_raw * lax.rsqrt(_segsum(q_raw * q_raw, bd_bf) + RMS_EPS) * (HEAD_DIM ** -0.5)
        k = k_raw * lax.rsqrt(_segsum(k_raw * k_raw, bd_bf) + RMS_EPS)
        gcol = _dot_c2(lincl_bf, g)
        grow = _dot_c2(ones_bf, g * ucat_ref[...])
        bcum = _dot_c2(lincl_bf, log_f)
        yield
        decay = jnp.exp2(jnp.where(incl, gcol - grow, NEG))
        k_bf = k.astype(BF16)
        kb_bf = blockdiag(k_bf)
        kk = _dot_nt(k_bf, kb_bf)
        qk = _dot_nt(q.astype(BF16), kb_bf)
        a_mat = jnp.where(strict, kk * decay * beta, 0.0)
        p_mat = qk * decay
        yield
        hbp_ref[s] = bcum
        hkp_ref[s] = k_c
        anchor = jnp.zeros((c, GROUP_W), F32)
        for t in range(1, c // SUB):
            anchor = jnp.where(sub == t, hbp_ref[s, t * SUB - 1:t * SUB, :], anchor)
        q_anch = (q_c * jnp.exp2(bcum - anchor)).astype(BF16)
        att = jnp.zeros((c, GROUP_W), F32)
        for t in range(1, c // SUB):
            a_t = hbp_ref[s, t * SUB - 1:t * SUB, :]
            k_anch = k_c * jnp.exp2(jnp.where(row < t * SUB, a_t - bcum, NEG))
            att = att + jnp.where(sub == t, _dot_nt(q_anch, blockdiag(k_anch.astype(BF16))), 0.0)
        yield
        t_mat = eye - a_mat
        pw = mmc(a_mat, a_mat)
        t_mat = t_mat + mmc(pw, t_mat)
        yield
        def key_rows(ref, jj):
            return jnp.concatenate(
                [jnp.broadcast_to(ref[s, t * SUB + jj:t * SUB + jj + 1, :], (SUB, GROUP_W))
                 for t in range(c // SUB)], axis=0)

        for jj in range(SUB):
            e = jnp.exp2(jnp.where(row_in_sub >= jj, bcum - key_rows(hbp_ref, jj), NEG))
            gs_ref[s, jj * c:(jj + 1) * c, :] = (q_c * e * key_rows(hkp_ref, jj)).astype(BF16)
        attb = _dot(gs_ref[s], bd_bf)
        yield
        pw = mmc(pw, pw)
        t_mat = t_mat + mmc(pw, t_mat)
        yield
        for jj in range(SUB):
            att = att + jnp.where(same_sub & (jcol_in_sub == jj), attb[jj * c:(jj + 1) * c, :], 0.0)
        o_c = _dot(att.astype(BF16), blockdiag(c_i.astype(BF16)))
        yield
        pw = mmc(pw, pw)
        t_mat = t_mat + mmc(pw, t_mat)
        yield
        pw = mmc(pw, pw)
        t_mat = t_mat + mmc(pw, t_mat)
        yield
        pw = mmc(pw, pw)
        t_mat = t_mat + mmc(pw, t_mat)
        egc = jnp.exp2(gcol)
        rhs = jnp.concatenate([blockdiag((v * beta).astype(BF16)),
                               blockdiag((k * beta * egc).astype(BF16))], axis=1)
        sol = _dot(t_mat.astype(BF16), rhs)
        u_sol = sol[:, 0:256]
        w_sol = sol[:, 256:512]
        yield
        sh = sh_ref[...]
        o_c = o_c + _dot_nt((q_c * jnp.exp2(bcum)).astype(BF16), sh.astype(BF16))
        blast = bcum[c - 1:c, :]
        kd_c = k_c * jnp.exp2(blast - bcum)
        sh_ref[...] = sh * jnp.exp2(blast) + _dot_tn(c_i.astype(BF16), kd_c.astype(BF16)) * bd
        ms_c = _segsum(o_c * o_c, bd_bf) * (1.0 / HEAD_DIM)
        y_ref[r0:r0 + c, 512:768] = (o_c * lax.rsqrt(ms_c + RMS_EPS) * prm_ref[3:4, :]
                                     * _silu(u2_ref[r0:r0 + c, 768:1024]))
        yield
        sg = sg_ref[...]
        sg_bf = sg.astype(BF16)
        v_new = u_sol - _dot(w_sol.astype(BF16), sg_bf)
        o_b = _dot((q * egc).astype(BF16), sg_bf) + _dot(p_mat.astype(BF16), blockdiag(v_new.astype(BF16)))
        glast = gcol[c - 1:c, :]
        kd = k * jnp.exp2(glast - gcol)
        sg_ref[...] = sg * jnp.exp2(glast) + _dot_tn(kd.astype(BF16), v_new.astype(BF16)) * bd
        ms_b = _segsum(o_b * o_b, bd_bf) * (1.0 / HEAD_DIM)
        y_ref[r0:r0 + c, 256:512] = (o_b * lax.rsqrt(ms_b + RMS_EPS) * prm_ref[2:3, :]
                                     * _silu(u0_ref[r0:r0 + c, C_BZ:C_BZ + 256]))

    stages = [chunk(s) for s in range(n_sub)]
    live = True
    while live:
        live = False
        for gen in stages:
            if next(gen, "done") != "done":
                live = True
    xpb_ref[0:HIST, :] = xpb_ref[rows:rows + HIST, :]

    @pl.when(ci == nc - 1)
    def _():
        sg_out_ref[0] = sg_ref[...]
        sh_out_ref[0] = sh_ref[...]


def _recur(u, l_valid, hist_a, hist_b, sg0, sh0, wa, wb, prm):
    bsz = sg0.shape[0]
    n = u.shape[0]
    n_sub = 1
    while n_sub < 4 and (n // bsz) % (2 * n_sub * CHUNK) == 0:
        n_sub *= 2
    rows = n_sub * CHUNK
    nc = n // bsz // rows
    consts = _recur_consts()
    cnames = ("lincl", "ucat", "incl", "strict", "eye", "bd")
    cvals = [jnp.asarray(consts[k]) for k in cnames]
    row_map = lambda col: (lambda b, ci: (b * nc + ci, col))
    full2 = lambda b, ci: (0, 0)
    per_b = lambda b, ci: (b, 0, 0)
    in_specs = [
        pl.BlockSpec((rows, 1024), row_map(0)),
        pl.BlockSpec((rows, 1024), row_map(1)),
        pl.BlockSpec((rows, 1024), row_map(2)),
        pl.BlockSpec((rows, 256), row_map(C_BB // 256)),
        pl.BlockSpec((1, HIST, 256), per_b),
        pl.BlockSpec((1, HIST, 768), per_b),
        pl.BlockSpec((1, 256, 256), per_b),
        pl.BlockSpec((1, 256, 256), per_b),
        pl.BlockSpec(wa.shape, full2),
        pl.BlockSpec(wb.shape, full2),
        pl.BlockSpec(prm.shape, full2),
    ] + [pl.BlockSpec(cv.shape, full2) for cv in cvals]
    out_shape = (jax.ShapeDtypeStruct((n, 768), F32),
                 jax.ShapeDtypeStruct((bsz, 256, 256), F32),
                 jax.ShapeDtypeStruct((bsz, 256, 256), F32),
                 jax.ShapeDtypeStruct((bsz, HIST, 256), F32))
    out_specs = (pl.BlockSpec((rows, 768), lambda b, ci: (b * nc + ci, 0)),
                 pl.BlockSpec((1, 256, 256), per_b),
                 pl.BlockSpec((1, 256, 256), per_b),
                 pl.BlockSpec((1, HIST, 256), per_b))
    scratch = [pltpu.VMEM((256, 256), F32), pltpu.VMEM((256, 256), F32),
               pltpu.VMEM((rows + HIST, 256), F32), pltpu.VMEM((rows + HIST, 768), F32),
               pltpu.VMEM((n_sub, CHUNK, 256), F32), pltpu.VMEM((n_sub, CHUNK, 256), F32),
               pltpu.VMEM((n_sub, SUB * CHUNK, 256), BF16)]
    assert n_sub == 1 or l_valid == CHUNK
    return pl.pallas_call(
        functools.partial(_recur_kernel, l_valid, n_sub),
        out_shape=out_shape,
        grid=(bsz, nc),
        in_specs=in_specs,
        out_specs=out_specs,
        scratch_shapes=scratch,
        compiler_params=pltpu.CompilerParams(
            dimension_semantics=("parallel", "arbitrary"), vmem_limit_bytes=VMEM_LIMIT),
        name="recur",
    )(u, u, u, u, hist_a, hist_b, sg0, sh0, wa, wb, prm, *cvals)


def _t5_bucket_np(dist):
    n = np.maximum(dist, 0)
    nf = np.maximum(n, MAX_EXACT).astype(np.float32)
    ratio = (np.log(nf / np.float32(MAX_EXACT)) / np.float32(math.log(MAX_DISTANCE / MAX_EXACT))).astype(np.float32)
    large = MAX_EXACT + (ratio * np.float32(N_BUCKETS - MAX_EXACT)).astype(np.int32)
    large = np.minimum(large, N_BUCKETS - 1)
    return np.where(n < MAX_EXACT, n, large).astype(np.int32)


def _bias_from_buckets(bkt, relb_ref, h):
    out = jnp.full(bkt.shape, NEG, F32)
    for b in range(N_BUCKETS):
        out = jnp.where(bkt == b, relb_ref[b, h] * LOG2E, out)
    return out


def _head_lane_mask(h, shape):
    lane = lax.broadcasted_iota(jnp.int32, shape, len(shape) - 1)
    return (lane >= h * HEAD_DIM) & (lane < (h + 1) * HEAD_DIM)


def _map_lane_mask(h, m, shape):
    lane = lax.broadcasted_iota(jnp.int32, shape, len(shape) - 1)
    lo = h * HEAD_DIM + m * HEAD_DIM_DQK
    return (lane >= lo) & (lane < lo + HEAD_DIM_DQK)


def _attn_finish(o1, o2, lam_ref, nd_ref, bd_bf):
    od = o1 - lam_ref[...] * o2
    ms = _segsum(od * od, bd_bf) * (1.0 / HEAD_DIM)
    return od * lax.rsqrt(ms + RMS_EPS) * nd_ref[...]


def _attn_prompt_kernel(qi_ref, kj_ref, relb_ref, q_ref, k_ref, v_ref, bkt_ref, lam_ref, nd_ref, bd_ref, o_ref,
                        q8_ref, m_ref, l_ref, acc_ref, bias_ref, s_ref, p_ref, al_ref):
    b = pl.program_id(0)
    step = pl.program_id(1)
    i = qi_ref[step]
    j = kj_ref[step]

    @pl.when((b == 0) & (step == 0))
    def _():
        for d in range(3):
            bkt = bkt_ref[d]
            for h in range(N_HEADS):
                bias_ref[d, h] = _bias_from_buckets(bkt, relb_ref, h)

    @pl.when(j == 0)
    def _():
        qt = (q_ref[...] * QK_SCALE_LOG2).T
        for hm in range(2 * N_HEADS):
            q8_ref[hm] = qt[hm * HEAD_DIM_DQK:(hm + 1) * HEAD_DIM_DQK, :].astype(BF16)
        m_ref[...] = jnp.full(m_ref.shape, NEG, F32)
        l_ref[...] = jnp.zeros(l_ref.shape, F32)
        acc_ref[...] = jnp.zeros(acc_ref.shape, F32)

    def update():
        vt = v_ref[...].T.astype(BF16)
        d = jnp.minimum(i - j, 2)
        for hm in range(2 * N_HEADS):
            k_hm = k_ref[:, hm * HEAD_DIM_DQK:(hm + 1) * HEAD_DIM_DQK].astype(BF16)
            s_ref[hm] = _dot(k_hm, q8_ref[hm]) + bias_ref[d, hm // 2]
        for hm in range(2 * N_HEADS):
            s = s_ref[hm]
            m_old = m_ref[hm]
            m_new = jnp.maximum(m_old, jnp.max(s, axis=0, keepdims=True))
            p = jnp.exp2(s - m_new)
            alpha = jnp.exp2(m_old - m_new)
            l_ref[hm] = alpha * l_ref[hm] + jnp.sum(p, axis=0, keepdims=True)
            p_ref[hm] = p.astype(BF16)
            al_ref[hm] = alpha
            m_ref[hm] = m_new
        for hm in range(2 * N_HEADS):
            h = hm // 2
            acc_ref[hm] = al_ref[hm] * acc_ref[hm] + _dot(vt[h * HEAD_DIM:(h + 1) * HEAD_DIM, :], p_ref[hm])

    update()

    @pl.when(j == i)
    def _():
        o1 = jnp.concatenate([acc_ref[2 * h] / l_ref[2 * h] for h in range(N_HEADS)], axis=0)
        o2 = jnp.concatenate([acc_ref[2 * h + 1] / l_ref[2 * h + 1] for h in range(N_HEADS)], axis=0)
        o_ref[...] = _attn_finish(o1.T, o2.T, lam_ref, nd_ref, bd_ref[...].astype(BF16))


def _attn_prompt(u, bsz, seq, rel_bias, lam_row, nd_row, bd):
    t = min(256, seq)
    nq = seq // t
    dist = [d * t + np.arange(t)[None, :] - np.arange(t)[:, None] for d in range(3)]
    bkt = np.stack([np.where(dd >= 0, _t5_bucket_np(dd), -1) for dd in dist]).astype(np.int32)
    assert nq <= 2 or t >= MAX_DISTANCE, "blocks two or more tiles back must share the last bucket"
    pairs = [(i, j) for i in range(nq) for j in range(i + 1)]
    qi = jnp.asarray(np.array([p[0] for p in pairs], np.int32))
    kj = jnp.asarray(np.array([p[1] for p in pairs], np.int32))
    full2 = lambda b, s, qi, kj: (0, 0)
    grid_spec = pltpu.PrefetchScalarGridSpec(
        num_scalar_prefetch=2,
        grid=(bsz, len(pairs)),
        in_specs=[
            pl.BlockSpec(memory_space=pltpu.SMEM),
            pl.BlockSpec((t, GROUP_W), lambda b, s, qi, kj: (b * nq + qi[s], C_DQ // 256)),
            pl.BlockSpec((t, GROUP_W), lambda b, s, qi, kj: (b * nq + kj[s], C_DK // 256)),
            pl.BlockSpec((t, GROUP_W), lambda b, s, qi, kj: (b * nq + kj[s], C_DV // 256)),
            pl.BlockSpec((3, t, t), lambda b, s, qi, kj: (0, 0, 0)),
            pl.BlockSpec((1, GROUP_W), full2),
            pl.BlockSpec((1, GROUP_W), full2),
            pl.BlockSpec((GROUP_W, GROUP_W), full2),
        ],
        out_specs=pl.BlockSpec((t, GROUP_W), lambda b, s, qi, kj: (b * nq + qi[s], 0)),
        scratch_shapes=[pltpu.VMEM((8, HEAD_DIM_DQK, t), BF16), pltpu.VMEM((8, 1, t), F32),
                        pltpu.VMEM((8, 1, t), F32), pltpu.VMEM((8, HEAD_DIM, t), F32),
                        pltpu.VMEM((3, N_HEADS, t, t), F32), pltpu.VMEM((8, t, t), F32),
                        pltpu.VMEM((8, t, t), BF16), pltpu.VMEM((8, 1, t), F32)],
    )
    return pl.pallas_call(
        _attn_prompt_kernel,
        out_shape=jax.ShapeDtypeStruct((bsz * seq, GROUP_W), F32),
        grid_spec=grid_spec,
        compiler_params=pltpu.CompilerParams(
            dimension_semantics=("arbitrary", "arbitrary"), vmem_limit_bytes=VMEM_LIMIT),
        name="attn_prompt",
    )(qi, kj, rel_bias, u, u, u, jnp.asarray(bkt), lam_row, nd_row, bd)


def _attn_sample_kernel(pp, n_steps, pt_ref, relb_ref, q_ref, kn_ref, vn_ref, *rest):
    k_refs = rest[:pp]
    v_refs = rest[pp:2 * pp]
    (bktl_ref, bktn_ref, lam_ref, nd_ref, o_ref,
     q16_ref, m_ref, l_ref, acc_ref, bias_ref, biasn_ref) = rest[2 * pp:]
    b = pl.program_id(0)
    s_id = pl.program_id(1)
    rows = 8
    page = k_refs[0].shape[-1]

    @pl.when((b == 0) & (s_id == 0))
    def _():
        for h in range(N_HEADS):
            near = _bias_from_buckets(bktl_ref[...], relb_ref, h)
            far = jnp.full((rows, page), relb_ref[N_BUCKETS - 1, h] * LOG2E, F32)
            newb = _bias_from_buckets(bktn_ref[...], relb_ref, h)
            for m in range(2):
                r0 = (2 * h + m) * rows
                for t in range(pp):
                    bias_ref[0, r0:r0 + rows, t * page:(t + 1) * page] = far
                    bias_ref[1, r0:r0 + rows, t * page:(t + 1) * page] = near if t == pp - 1 else far
                biasn_ref[r0:r0 + rows, :] = newb

    hrows = 2 * rows

    @pl.when(s_id == 0)
    def _():
        lane = lax.broadcasted_iota(jnp.int32, (rows, HEAD_DIM), 1)
        for h in range(N_HEADS):
            qh = q_ref[0, h] * QK_SCALE_LOG2
            q16_ref[h * hrows:h * hrows + rows, :] = jnp.where(lane < HEAD_DIM_DQK, qh, 0.0)
            q16_ref[h * hrows + rows:(h + 1) * hrows, :] = jnp.where(lane >= HEAD_DIM_DQK, qh, 0.0)
        s = jnp.concatenate(
            [_dot_nt(q16_ref[h * hrows:(h + 1) * hrows, :].astype(BF16), kn_ref[0, h].astype(BF16))
             for h in range(N_HEADS)], axis=0) + biasn_ref[...]
        m0 = jnp.max(s, axis=1, keepdims=True)
        p = jnp.exp2(s - m0)
        m_ref[...] = m0
        l_ref[...] = jnp.sum(p, axis=1, keepdims=True)
        for h in range(N_HEADS):
            acc_ref[h * hrows:(h + 1) * hrows, :] = _dot(
                p[h * hrows:(h + 1) * hrows, :].astype(BF16), vn_ref[0, h].astype(BF16))

    sel = jnp.where(s_id == n_steps - 1, 1, 0)
    s = jnp.concatenate(
        [_dot(q16_ref[h * hrows:(h + 1) * hrows, :].astype(BF16),
              jnp.concatenate([r[0, 0, h] for r in k_refs], axis=1).astype(BF16))
         for h in range(N_HEADS)], axis=0) + bias_ref[sel]
    m_old = m_ref[...]
    m_new = jnp.maximum(m_old, jnp.max(s, axis=1, keepdims=True))
    p32 = jnp.exp2(s - m_new)
    alpha = jnp.exp2(m_old - m_new)
    l_ref[...] = alpha * l_ref[...] + jnp.sum(p32, axis=1, keepdims=True)
    p = p32.astype(BF16)
    pv = jnp.concatenate(
        [_dot_nt(p[h * hrows:(h + 1) * hrows, :],
                 jnp.concatenate([r[0, 0, h] for r in v_refs], axis=1).astype(BF16))
         for h in range(N_HEADS)], axis=0)
    acc_ref[...] = alpha * acc_ref[...] + pv
    m_ref[...] = m_new

    @pl.when(s_id == n_steps - 1)
    def _():
        o = acc_ref[...] / l_ref[...]
        for h in range(N_HEADS):
            od = o[h * hrows:h * hrows + rows, :] - lam_ref[...] * o[h * hrows + rows:(h + 1) * hrows, :]
            ms = jnp.mean(od * od, axis=1, keepdims=True)
            o_ref[0, h] = od * lax.rsqrt(ms + RMS_EPS) * nd_ref[...]


def _attn_sample(layer, q4, k4, v4, ck, cv, page_table, l_new, rel_bias, lam_row, nd_row):
    bsz = q4.shape[0]
    n_pages = page_table.shape[1]
    page = ck.shape[-1]
    pp = 16
    while n_pages % pp:
        pp //= 2
    n_steps = n_pages // pp
    past = n_pages * page
    qpos = past + np.arange(8)[:, None]
    bkt_last = _t5_bucket_np(qpos - (past - page + np.arange(page))[None, :])
    dist_new = np.arange(8)[:, None] - np.arange(8)[None, :]
    ok = (dist_new >= 0) & (np.arange(8)[None, :] < l_new)
    bkt_new = np.where(ok, _t5_bucket_np(dist_new), -1).astype(np.int32)
    assert page >= MAX_DISTANCE, "only the last cache page may need distance-dependent bias"

    def page_map(t):
        return lambda b, s, pt: (layer, pt[b, s * pp + t], 0, 0, 0)

    per_b = lambda b, s, pt: (b, 0, 0, 0)
    full2 = lambda b, s, pt: (0, 0)
    new_spec = pl.BlockSpec((1, N_HEADS, 8, HEAD_DIM), per_b)
    page_spec = lambda t: pl.BlockSpec((1, 1, N_HEADS, HEAD_DIM, page), page_map(t))
    in_specs = ([pl.BlockSpec(memory_space=pltpu.SMEM), new_spec, new_spec, new_spec]
                + [page_spec(t) for t in range(pp)]
                + [page_spec(t) for t in range(pp)]
                + [pl.BlockSpec((8, page), full2), pl.BlockSpec((8, 8), full2),
                   pl.BlockSpec((1, HEAD_DIM), full2), pl.BlockSpec((1, HEAD_DIM), full2)])
    grid_spec = pltpu.PrefetchScalarGridSpec(
        num_scalar_prefetch=1,
        grid=(bsz, n_steps),
        in_specs=in_specs,
        out_specs=new_spec,
        scratch_shapes=[pltpu.VMEM((N_HEADS * 16, HEAD_DIM), F32), pltpu.VMEM((N_HEADS * 16, 1), F32),
                        pltpu.VMEM((N_HEADS * 16, 1), F32), pltpu.VMEM((N_HEADS * 16, HEAD_DIM), F32),
                        pltpu.VMEM((2, N_HEADS * 16, pp * page), F32), pltpu.VMEM((N_HEADS * 16, 8), F32)],
    )
    return pl.pallas_call(
        functools.partial(_attn_sample_kernel, pp, n_steps),
        out_shape=jax.ShapeDtypeStruct((bsz, N_HEADS, 8, HEAD_DIM), F32),
        grid_spec=grid_spec,
        compiler_params=pltpu.CompilerParams(
            dimension_semantics=("arbitrary", "arbitrary"), vmem_limit_bytes=VMEM_LIMIT),
        name="attn_sample",
    )(page_table, rel_bias, q4, k4, v4, *([ck] * pp), *([cv] * pp),
      jnp.asarray(bkt_last), jnp.asarray(bkt_new), lam_row[:, :HEAD_DIM], nd_row[:, :HEAD_DIM])


def _oproj_kernel(alpha, yabc_ref, yd_ref, x_ref, wo_ref, g_ref, b_ref, o_ref):
    mix = (_dot(yabc_ref[...].astype(BF16), wo_ref[0:768, :])
           + _dot(yd_ref[...].astype(BF16), wo_ref[768:1024, :]))
    o_ref[...] = _layer_norm(alpha * x_ref[...] + mix, g_ref[...], b_ref[...])


def _oproj(yabc, yd, x, wo, g, b, alpha, tm):
    n = x.shape[0]
    row = lambda i: (i, 0)
    full = lambda i: (0, 0)
    return pl.pallas_call(
        functools.partial(_oproj_kernel, alpha),
        out_shape=jax.ShapeDtypeStruct((n, D_MODEL), F32),
        grid=(n // tm,),
        in_specs=[pl.BlockSpec((tm, 768), row), pl.BlockSpec((tm, 256), row),
                  pl.BlockSpec((tm, D_MODEL), row), pl.BlockSpec((D_MODEL, D_MODEL), full),
                  pl.BlockSpec((1, D_MODEL), full), pl.BlockSpec((1, D_MODEL), full)],
        out_specs=pl.BlockSpec((tm, D_MODEL), row),
        compiler_params=pltpu.CompilerParams(
            dimension_semantics=("parallel",), vmem_limit_bytes=VMEM_LIMIT),
        name="out_proj_ln",
    )(yabc, yd, x, wo, g, b)


def _ffn_kernel(alpha, x_ref, wg_ref, wu_ref, wd_ref, g_ref, b_ref, o_ref, xb_ref, acc_ref):
    f = pl.program_id(1)

    @pl.when(f == 0)
    def _():
        xb_ref[...] = x_ref[...].astype(BF16)
        acc_ref[...] = jnp.zeros(acc_ref.shape, F32)

    xb = xb_ref[...]
    h = _silu(_dot(xb, wg_ref[...])) * _dot(xb, wu_ref[...])
    acc_ref[...] += _dot(h.astype(BF16), wd_ref[...])

    @pl.when(f == pl.num_programs(1) - 1)
    def _():
        o_ref[...] = _layer_norm(alpha * x_ref[...] + acc_ref[...], g_ref[...], b_ref[...])


def _ffn(x, wg, wu, wd, g, b, alpha, tm, tf):
    n = x.shape[0]
    ff = wg.shape[1]
    row = lambda i, f: (i, 0)
    full = lambda i, f: (0, 0)
    return pl.pallas_call(
        functools.partial(_ffn_kernel, alpha),
        out_shape=jax.ShapeDtypeStruct((n, D_MODEL), F32),
        grid=(n // tm, ff // tf),
        in_specs=[pl.BlockSpec((tm, D_MODEL), row),
                  pl.BlockSpec((D_MODEL, tf), lambda i, f: (0, f)),
                  pl.BlockSpec((D_MODEL, tf), lambda i, f: (0, f)),
                  pl.BlockSpec((tf, D_MODEL), lambda i, f: (f, 0)),
                  pl.BlockSpec((1, D_MODEL), full), pl.BlockSpec((1, D_MODEL), full)],
        out_specs=pl.BlockSpec((tm, D_MODEL), row),
        scratch_shapes=[pltpu.VMEM((tm, D_MODEL), BF16), pltpu.VMEM((tm, D_MODEL), F32)],
        compiler_params=pltpu.CompilerParams(
            dimension_semantics=("parallel", "arbitrary"), vmem_limit_bytes=VMEM_LIMIT),
        name="ffn_ln",
    )(x, wg, wu, wd, g, b)


def _moe_kernel(alpha, rb, x_ref, wrt_ref, wg_ref, wu_ref, wd_ref, g_ref, b_ref, o_ref,
                xb_ref, sel_ref, rank_ref, gate_ref, upper_ref):
    e = pl.program_id(1)
    tile = x_ref.shape[0]

    @pl.when(e == 0)
    def _():
        x = x_ref[...]
        xb_ref[...] = x.astype(BF16)
        o_ref[...] = jnp.zeros(o_ref.shape, F32)
        r_i = lax.broadcasted_iota(jnp.int32, (tile, tile), 0)
        c_i = lax.broadcasted_iota(jnp.int32, (tile, tile), 1)
        upper_ref[...] = jnp.where(r_i < c_i, 1.0, 0.0).astype(BF16)
        logits = lax.dot_general(wrt_ref[...], x, (((1,), (1,)), ((), ())),
                                 preferred_element_type=F32, precision=lax.Precision.HIGHEST)
        eid = lax.broadcasted_iota(jnp.int32, logits.shape, 0)
        logits = jnp.where(eid < N_EXPERTS, logits, NEG)
        big = logits.shape[0]
        m1 = jnp.max(logits, axis=0, keepdims=True)
        i1 = jnp.min(jnp.where(logits == m1, eid, big), axis=0, keepdims=True)
        rest = jnp.where(eid == i1, NEG, logits)
        m2 = jnp.max(rest, axis=0, keepdims=True)
        i2 = jnp.min(jnp.where(rest == m2, eid, big), axis=0, keepdims=True)
        e2 = jnp.exp(m2 - m1)
        g1 = 1.0 / (1.0 + e2)
        g2 = e2 / (1.0 + e2)
        sel = jnp.where((eid == i1) | (eid == i2), 1.0, 0.0)
        sel_ref[...] = sel
        gate_ref[...] = jnp.where(eid == i1, g1, 0.0) + jnp.where(eid == i2, g2, 0.0)
        rank_ref[...] = _dot(sel.astype(BF16), upper_ref[...])

    sel_row = sel_ref[pl.ds(e, 1), :]
    rank_row = rank_ref[pl.ds(e, 1), :]
    gate_row = gate_ref[pl.ds(e, 1), :]
    count = jnp.sum(sel_row).astype(jnp.int32)
    slot = lax.broadcasted_iota(jnp.int32, (rb, tile), 0).astype(F32)

    def block(blk, carry):
        base = (blk * rb).astype(F32)
        onehot = (slot + base == rank_row) & (sel_row > 0.5)
        gather = jnp.where(onehot, 1.0, 0.0).astype(BF16)
        xg = _dot(gather, xb_ref[...]).astype(BF16)
        h = _silu(_dot(xg, wg_ref[0])) * _dot(xg, wu_ref[0])
        y = _dot(h.astype(BF16), wd_ref[0])
        y = y * jnp.sum(jnp.where(onehot, gate_row, 0.0), axis=1, keepdims=True)
        yh, yl = _split2(y)
        o_ref[...] += _dot_tn(gather, yh) + _dot_tn(gather, yl)
        return carry

    lax.fori_loop(0, (count + rb - 1) // rb, block, 0)

    @pl.when(e == pl.num_programs(1) - 1)
    def _():
        o_ref[...] = _layer_norm(alpha * x_ref[...] + o_ref[...], g_ref[...], b_ref[...])


def _moe(x, wrt, wg, wu, wd, g, b, alpha, tile, rb):
    n = x.shape[0]
    n_e, _, ff = wg.shape
    row = lambda i, e: (i, 0)
    full = lambda i, e: (0, 0)
    per_e = lambda i, e: (e, 0, 0)
    return pl.pallas_call(
        functools.partial(_moe_kernel, alpha, rb),
        out_shape=jax.ShapeDtypeStruct((n, D_MODEL), F32),
        grid=(n // tile, n_e),
        in_specs=[pl.BlockSpec((tile, D_MODEL), row),
                  pl.BlockSpec(wrt.shape, full),
                  pl.BlockSpec((1, D_MODEL, ff), per_e),
                  pl.BlockSpec((1, D_MODEL, ff), per_e),
                  pl.BlockSpec((1, ff, D_MODEL), per_e),
                  pl.BlockSpec((1, D_MODEL), full), pl.BlockSpec((1, D_MODEL), full)],
        out_specs=pl.BlockSpec((tile, D_MODEL), row),
        scratch_shapes=[pltpu.VMEM((tile, D_MODEL), BF16), pltpu.VMEM((16, tile), F32),
                        pltpu.VMEM((16, tile), F32), pltpu.VMEM((16, tile), F32),
                        pltpu.VMEM((tile, tile), BF16)],
        compiler_params=pltpu.CompilerParams(
            dimension_semantics=("parallel", "arbitrary"), vmem_limit_bytes=VMEM_LIMIT),
        name="moe_ln",
    )(x, wrt, wg, wu, wd, g, b)


def _w_in_columns():
    o = {}
    off = 0
    for name, size in (("a_in", 256), ("a_gb", 256), ("a_gc", 256), ("b_qkv", 768), ("b_a", 4), ("b_b", 4),
                       ("b_z", 256), ("c_q", 256), ("c_f", 256), ("c_i", 256), ("c_g", 256),
                       ("d_q", 256), ("d_k", 256), ("d_v", 256)):
        o[name] = np.arange(off, off + size)
        off += size
    rep = lambda ix: np.repeat(ix, HEAD_DIM)
    cols = np.concatenate([o["a_in"], o["a_gb"], o["a_gc"], o["b_z"],
                           o["b_qkv"], rep(o["b_a"]),
                           o["c_q"], o["c_f"], o["c_i"], o["c_g"],
                           rep(o["b_b"]), o["d_q"], o["d_k"], o["d_v"]])
    assert cols.shape[0] == U_W
    return cols


def _pick_tile(n, pref):
    t = min(pref, n)
    while n % t:
        t //= 2
    return t


def _blockdiag_state(s):
    bsz = s.shape[0]
    eye = jnp.eye(N_HEADS, dtype=s.dtype)
    return (s[:, :, :, None, :] * eye[None, :, None, :, None]).reshape(bsz, GROUP_W, GROUP_W)


def _diag_blocks(s):
    bsz = s.shape[0]
    s5 = s.reshape(bsz, N_HEADS, HEAD_DIM, N_HEADS, HEAD_DIM)
    return jnp.stack([s5[:, h, :, h, :] for h in range(N_HEADS)], axis=1)


def _run_group(x, init, attend, lw, depth):
    bsz, seq, _ = x.shape
    lp = -(-seq // CHUNK) * CHUNK
    assert (lp == seq or seq < CHUNK) and seq >= CONV_B_W - 1
    n = bsz * seq
    alpha = (2 * depth) ** 0.25
    xf = x.reshape(n, D_MODEL)
    tm = _pick_tile(n, 512)
    outs = []
    for l in range(depth):
        w = lw[l]
        u = _proj(xf, w["w_in"], tm, 1024)
        u3 = u.reshape(bsz, seq, U_W)
        if init is None:
            hist_a = jnp.zeros((bsz, HIST, 256), F32)
            hist_b = jnp.zeros((bsz, HIST, 768), F32)
            sg0 = jnp.zeros((bsz, GROUP_W, GROUP_W), F32)
            sh0 = sg0
        else:
            ca, cb, s_g, s_h = init[l]
            hist_a = jnp.pad(ca, ((0, 0), (HIST - ca.shape[1], 0), (0, 0)))
            hist_b = jnp.pad(cb, ((0, 0), (HIST - cb.shape[1], 0), (0, 0)))
            sg0 = _blockdiag_state(s_g)
            sh0 = _blockdiag_state(jnp.swapaxes(s_h, 2, 3))
        u_rec = u if lp == seq else jnp.pad(u3, ((0, 0), (0, lp - seq), (0, 0))).reshape(bsz * lp, U_W)
        yabc, sg1, sh1, tail_a = _recur(u_rec, min(seq, CHUNK) if lp != seq else CHUNK,
                                        hist_a, hist_b, sg0, sh0, w["conv_a"], w["conv_b"], w["prm"])
        if lp != seq:
            yabc = yabc.reshape(bsz, lp, 768)[:, :seq].reshape(n, 768)
        yd = attend(l, u, u3, w)
        x1 = _oproj(yabc, yd, xf, w["w_o"], w["ln1_g"], w["ln1_b"], alpha, tm)
        f = w["ffn"]
        if f["moe"]:
            tile = _pick_tile(n, 1024)
            rb = -(-(tile * 9 // 32) // 16) * 16
            xf = _moe(x1, f["wrt"], f["wg"], f["wu"], f["wd"], w["ln2_g"], w["ln2_b"], alpha, tile, rb)
        else:
            xf = _ffn(x1, f["wg"], f["wu"], f["wd"], w["ln2_g"], w["ln2_b"], alpha,
                      _pick_tile(n, 512), f["tf"])
        a0 = HIST - (CONV_A_W - 1)
        outs.append((
            u3[:, :, C_DK:C_DK + 256].reshape(bsz, seq, N_HEADS, HEAD_DIM),
            u3[:, :, C_DV:C_DV + 256].reshape(bsz, seq, N_HEADS, HEAD_DIM),
            tail_a[:, a0:a0 + CONV_A_W - 1],
            u3[:, seq - (CONV_B_W - 1):, C_BQKV:C_BQKV + 768],
            _diag_blocks(sg1),
            jnp.swapaxes(_diag_blocks(sh1), 2, 3),
        ))
    return xf.reshape(bsz, seq, D_MODEL), [jnp.stack([o[i] for o in outs]) for i in range(6)]


def kernel(x_prompt, x_sample, cache_k, cache_v, state_conv_a, state_conv_b, state_gdn, state_hgrn, page_table, w_in, conv_a, conv_b, gdn_a_log, gdn_dt_bias, norm_b, lower_bounds, norm_c, lambda_q1, lambda_k1, lambda_q2, lambda_k2, norm_d, rel_bias, w_o, ln1_g, ln1_b, ffn_w_gate, ffn_w_up, ffn_w_down, router_w, moe_w_gate, moe_w_up, moe_w_down, ln2_g, ln2_b):
    depth = w_in.shape[0]
    cols = _w_in_columns()
    rep = lambda t: jnp.repeat(t.astype(F32), HEAD_DIM)
    tile4 = lambda t: jnp.tile(t.astype(F32), N_HEADS)
    lbs = jax.nn.softmax(lower_bounds.astype(F32), axis=0)
    lb_all = jnp.cumsum(lbs, axis=0) - lbs[0]
    bd = jnp.asarray(_recur_consts()["bd"])
    lw = []
    for l in range(depth):
        lam_init = 0.8 - 0.6 * math.exp(-0.3 * l)
        lam = (jnp.exp(jnp.sum(lambda_q1[l].astype(F32) * lambda_k1[l].astype(F32)))
               - jnp.exp(jnp.sum(lambda_q2[l].astype(F32) * lambda_k2[l].astype(F32))) + lam_init)
        prm = jnp.stack([-jnp.exp(rep(gdn_a_log[l])) * LOG2E, rep(gdn_dt_bias[l]), tile4(norm_b[l]),
                         tile4(norm_c[l]), lb_all[l], jnp.zeros((GROUP_W,), F32),
                         jnp.zeros((GROUP_W,), F32), jnp.zeros((GROUP_W,), F32)])
        j = l // 2
        if l % 2 == 0:
            ff = ffn_w_gate.shape[2]
            ffn = dict(moe=False, wg=ffn_w_gate[j].astype(BF16), wu=ffn_w_up[j].astype(BF16),
                       wd=ffn_w_down[j].astype(BF16))
        else:
            ff = moe_w_gate.shape[3]
            ffn = dict(moe=True, wrt=jnp.pad(router_w[j].astype(F32).T, ((0, 16 - N_EXPERTS), (0, 0))),
                       wg=moe_w_gate[j].astype(BF16), wu=moe_w_up[j].astype(BF16),
                       wd=moe_w_down[j].astype(BF16))
        tf = ff
        for cand in (1408, 1024, 512, 256, 128):
            if ff % cand == 0:
                tf = cand
                break
        ffn["tf"] = tf
        lw.append(dict(
            w_in=w_in[l][:, cols].astype(BF16),
            conv_a=jnp.pad(conv_a[l].astype(F32), ((0, HIST - CONV_A_W), (0, 0))),
            conv_b=jnp.pad(conv_b[l].astype(F32), ((0, HIST - CONV_B_W), (0, 0))),
            prm=prm,
            lam_row=jnp.full((1, GROUP_W), lam, F32),
            nd_row=(tile4(norm_d[l]) * (1.0 - lam_init))[None, :],
            w_o=w_o[l].astype(BF16),
            ln1_g=ln1_g[l][None].astype(F32), ln1_b=ln1_b[l][None].astype(F32),
            ln2_g=ln2_g[l][None].astype(F32), ln2_b=ln2_b[l][None].astype(F32),
            ffn=ffn,
        ))
    relb = rel_bias.astype(F32)

    def attend_prompt(l, u, u3, w):
        bsz, seq, _ = u3.shape
        return _attn_prompt(u, bsz, seq, relb, w["lam_row"], w["nd_row"], bd)

    ck_t = jnp.transpose(cache_k, (0, 1, 3, 4, 2))
    cv_t = jnp.transpose(cache_v, (0, 1, 3, 4, 2))

    def attend_sample(l, u, u3, w):
        bsz, seq, _ = u3.shape

        def heads8(c0):
            t = u3[:, :, c0:c0 + 256].reshape(bsz, seq, N_HEADS, HEAD_DIM)
            return jnp.pad(jnp.swapaxes(t, 1, 2), ((0, 0), (0, 0), (0, 8 - seq), (0, 0)))

        o = _attn_sample(l, heads8(C_DQ), heads8(C_DK), heads8(C_DV), ck_t, cv_t,
                         page_table, seq, relb, w["lam_row"], w["nd_row"])
        return jnp.swapaxes(o[:, :, :seq], 1, 2).reshape(bsz * seq, GROUP_W)

    init_s = [(state_conv_a[l], state_conv_b[l], state_gdn[l], state_hgrn[l]) for l in range(depth)]
    y_p, (k_p, v_p, ca_p, cb_p, sg_p, sh_p) = _run_group(x_prompt, None, attend_prompt, lw, depth)
    y_s, (k_s, v_s, ca_s, cb_s, sg_s, sh_s) = _run_group(x_sample, init_s, attend_sample, lw, depth)
    return (y_p, y_s, k_p, v_p, k_s, v_s, ca_p, ca_s, cb_p, cb_s, sg_p, sg_s, sh_p, sh_s)
```

```python
import functools
import math

import numpy as np
import jax
import jax.numpy as jnp
from jax import lax
from jax.experimental import pallas as pl
from jax.experimental.pallas import tpu as pltpu

F32 = jnp.float32
BF16 = jnp.bfloat16

D_MODEL = 1024
GROUP_W = 256
N_HEADS = 4
HEAD_DIM = 64
CHUNK = 64
HIST = 8
SUB = 16
SUB_SHIFT = 4
CONV_A_W = 3
CONV_B_W = 4
HEAD_DIM_DQK = 32
N_BUCKETS = 32
MAX_EXACT = 16
MAX_DISTANCE = 128
N_EXPERTS = 8
LN_EPS = 1e-5
RMS_EPS = 1e-6
NEG = -1e30
LOG2E = math.log2(math.e)
QK_SCALE_LOG2 = HEAD_DIM_DQK ** -0.5 * LOG2E
U_W = 4096
VMEM_LIMIT = 56 * 1024 * 1024
MOE_VMEM_LIMIT = 60 * 1024 * 1024

C_AIN, C_AGB, C_AGC, C_BZ = 0, 256, 512, 768
C_BQKV, C_BA = 1024, 1792
C_CQ, C_CF, C_CI, C_CG = 2048, 2304, 2560, 2816
C_BB, C_DQ, C_DK, C_DV = 3072, 3328, 3584, 3840


def _dot(a, b):
    return jnp.dot(a, b, preferred_element_type=F32)


def _dot_nt(a, b):
    return lax.dot_general(a, b, (((1,), (1,)), ((), ())), preferred_element_type=F32)


def _dot_tn(a, b):
    return lax.dot_general(a, b, (((0,), (0,)), ((), ())), preferred_element_type=F32)


def _split2(x):
    hi = x.astype(BF16)
    lo = (x - hi.astype(F32)).astype(BF16)
    return hi, lo


def _dot_c2(c_bf, x):
    hi, lo = _split2(x)
    return _dot(c_bf, hi) + _dot(c_bf, lo)


def _segsum(s, bd_bf):
    return _dot(s.astype(BF16), bd_bf)


def _sigmoid(x):
    return 0.5 * jnp.tanh(0.5 * x) + 0.5


def _silu(x):
    return x * _sigmoid(x)


def _softplus(x):
    return jnp.maximum(x, 0.0) + jnp.log(1.0 + jnp.exp(-jnp.abs(x)))


def _layer_norm(z, g, b):
    mu = jnp.mean(z, axis=-1, keepdims=True)
    zc = z - mu
    var = jnp.mean(zc * zc, axis=-1, keepdims=True)
    return zc * lax.rsqrt(var + LN_EPS) * g + b


def _proj_kernel(tn, x_ref, w_ref, o_ref):
    xb = x_ref[...].astype(BF16)
    for c0 in range(0, w_ref.shape[1], tn):
        o_ref[:, c0:c0 + tn] = _dot(xb, w_ref[:, c0:c0 + tn])


def _proj(x, w, tm, tn):
    n, k = x.shape
    m = w.shape[1]
    return pl.pallas_call(
        functools.partial(_proj_kernel, tn),
        out_shape=jax.ShapeDtypeStruct((n, m), F32),
        grid=(n // tm,),
        in_specs=[pl.BlockSpec((tm, k), lambda i: (i, 0)),
                  pl.BlockSpec((k, m), lambda i: (0, 0))],
        out_specs=pl.BlockSpec((tm, m), lambda i: (i, 0)),
        compiler_params=pltpu.CompilerParams(
            dimension_semantics=("parallel",), vmem_limit_bytes=VMEM_LIMIT),
        name="proj_in",
    )(x, w)


def _recur_consts():
    c = CHUNK
    w = GROUP_W
    r = np.arange(c)[:, None]
    lane = np.arange(w)[None, :]
    j = lane % c
    bd = (np.arange(w)[:, None] // c == lane // c)
    return dict(
        lincl=(np.arange(c)[None, :] <= r).astype(np.float32),
        ucat=(r <= j).astype(np.float32),
        incl=(j <= r).astype(np.float32),
        strict=(j < r).astype(np.float32),
        eye=(j == r).astype(np.float32),
        bd=bd.astype(np.float32),
    )


def _recur_kernel(l_valid, n_sub, u0_ref, u1_ref, u2_ref, ubb_ref, hista_ref, histb_ref, sg0_ref, sh0_ref,
                  wa_ref, wb_ref, prm_ref, lincl_ref, ucat_ref, incl_ref, strict_ref, eye_ref, bd_ref,
                  y_ref, sg_out_ref, sh_out_ref, hista_out_ref,
                  sg_ref, sh_ref, xpa_ref, xpb_ref, hbp_ref, hkp_ref, gs_ref):
    c = CHUNK
    rows = n_sub * c
    ci = pl.program_id(1)
    nc = pl.num_programs(1)

    @pl.when(ci == 0)
    def _():
        sg_ref[...] = sg0_ref[0]
        sh_ref[...] = sh0_ref[0]
        xpa_ref[0:HIST, :] = hista_ref[0]
        xpb_ref[0:HIST, :] = histb_ref[0]

    bd = bd_ref[...]
    bd_bf = bd.astype(BF16)
    lincl_bf = lincl_ref[...].astype(BF16)
    ones_bf = jnp.ones((c, c), BF16)
    incl = incl_ref[...] > 0.5
    strict = strict_ref[...] > 0.5
    eye = eye_ref[...]
    row = lax.broadcasted_iota(jnp.int32, (c, GROUP_W), 0)
    jcol = lax.broadcasted_iota(jnp.int32, (c, GROUP_W), 1) & (c - 1)
    sub = row >> SUB_SHIFT
    row_in_sub = row & (SUB - 1)
    same_sub = (jcol >> SUB_SHIFT) == sub
    jcol_in_sub = jcol & (SUB - 1)

    def blockdiag(x):
        return jnp.concatenate([x, x, x, x], axis=0) * (bd_bf if x.dtype == BF16 else bd)

    def mmc(x, y):
        return _dot(x.astype(BF16), blockdiag(y.astype(BF16)))

    def mask_rows(x):
        return x if l_valid >= c else jnp.where(row < l_valid, x, 0.0)

    xpa_ref[HIST:HIST + rows, :] = u0_ref[:, C_AGC:C_AGC + 256] * u0_ref[:, C_AIN:C_AIN + 256]
    conv_a = jnp.zeros((rows, GROUP_W), F32)
    for jj in range(CONV_A_W):
        off = HIST - (CONV_A_W - 1) + jj
        conv_a = conv_a + xpa_ref[off:off + rows, :] * wa_ref[jj:jj + 1, :]
    y_ref[:, 0:256] = u0_ref[:, C_AGB:C_AGB + 256] * conv_a
    last = l_valid if n_sub == 1 else rows
    hista_out_ref[0] = xpa_ref[last:last + HIST, :]
    xpa_ref[0:HIST, :] = xpa_ref[rows:rows + HIST, :]
    xpb_ref[HIST:HIST + rows, :] = u1_ref[:, 0:768]

    def chunk(s):
        r0 = s * c
        conv_b = jnp.zeros((c, 768), F32)
        for jj in range(CONV_B_W):
            off = HIST - (CONV_B_W - 1) + jj + r0
            conv_b = conv_b + xpb_ref[off:off + c, :] * wb_ref[jj:jj + 1, :]
        conv_b = _silu(conv_b)
        q_raw = conv_b[:, 0:256]
        k_raw = conv_b[:, 256:512]
        v = conv_b[:, 512:768]
        beta = mask_rows(_sigmoid(ubb_ref[r0:r0 + c, :]))
        g = mask_rows(prm_ref[0:1, :] * _softplus(u1_ref[r0:r0 + c, 768:1024] + prm_ref[1:2, :]))
        lb = prm_ref[4:5, :]
        sig = _sigmoid(u2_ref[r0:r0 + c, 256:512])
        log_f = mask_rows(jnp.log2(lb + (1.0 - lb) * sig))
        k_c = mask_rows((1.0 - lb) * (1.0 - sig))
        q_c = _silu(u2_ref[r0:r0 + c, 0:256])
        c_i = u2_ref[r0:r0 + c, 512:768]
        yield
        q = q_raw * lax.rsqrt(_segsum(q_raw * q_raw, bd_bf) + RMS_EPS) * (HEAD_DIM ** -0.5)
        k = k_raw * lax.rsqrt(_segsum(k_raw * k_raw, bd_bf) + RMS_EPS)
        gcol = _dot_c2(lincl_bf, g)
        grow = _dot_c2(ones_bf, g * ucat_ref[...])
        bcum = _dot_c2(lincl_bf, log_f)
        yield
        decay = jnp.exp2(jnp.where(incl, gcol - grow, NEG))
        k_bf = k.astype(BF16)
        kb_bf = blockdiag(k_bf)
        kk = _dot_nt(k_bf, kb_bf)
        qk = _dot_nt(q.astype(BF16), kb_bf)
        a_mat = jnp.where(strict, kk * decay * beta, 0.0)
        p_mat = qk * decay
        yield
        hbp_ref[s] = bcum
        hkp_ref[s] = k_c
        anchor = jnp.zeros((c, GROUP_W), F32)
        for t in range(1, c // SUB):
            anchor = jnp.where(sub == t, hbp_ref[s, t * SUB - 1:t * SUB, :], anchor)
        q_anch = (q_c * jnp.exp2(bcum - anchor)).astype(BF16)
        att = jnp.zeros((c, GROUP_W), F32)
        for t in range(1, c // SUB):
            a_t = hbp_ref[s, t * SUB - 1:t * SUB, :]
            k_anch = k_c * jnp.exp2(jnp.where(row < t * SUB, a_t - bcum, NEG))
            att = att + jnp.where(sub == t, _dot_nt(q_anch, blockdiag(k_anch.astype(BF16))), 0.0)
        yield
        t_mat = eye - a_mat
        pw = mmc(a_mat, a_mat)
        t_mat = t_mat + mmc(pw, t_mat)
        yield
        def key_rows(ref, jj):
            return jnp.concatenate(
                [jnp.broadcast_to(ref[s, t * SUB + jj:t * SUB + jj + 1, :], (SUB, GROUP_W))
                 for t in range(c // SUB)], axis=0)

        for jj in range(SUB):
            e = jnp.exp2(jnp.where(row_in_sub >= jj, bcum - key_rows(hbp_ref, jj), NEG))
            gs_ref[s, jj * c:(jj + 1) * c, :] = (q_c * e * key_rows(hkp_ref, jj)).astype(BF16)
        attb = _dot(gs_ref[s], bd_bf)
        yield
        pw = mmc(pw, pw)
        t_mat = t_mat + mmc(pw, t_mat)
        yield
        for jj in range(SUB):
            att = att + jnp.where(same_sub & (jcol_in_sub == jj), attb[jj * c:(jj + 1) * c, :], 0.0)
        o_c = _dot(att.astype(BF16), blockdiag(c_i.astype(BF16)))
        yield
        pw = mmc(pw, pw)
        t_mat = t_mat + mmc(pw, t_mat)
        yield
        pw = mmc(pw, pw)
        t_mat = t_mat + mmc(pw, t_mat)
        yield
        pw = mmc(pw, pw)
        t_mat = t_mat + mmc(pw, t_mat)
        egc = jnp.exp2(gcol)
        rhs = jnp.concatenate([blockdiag((v * beta).astype(BF16)),
                               blockdiag((k * beta * egc).astype(BF16))], axis=1)
        sol = _dot(t_mat.astype(BF16), rhs)
        u_sol = sol[:, 0:256]
        w_sol = sol[:, 256:512]
        yield
        sh = sh_ref[...]
        o_c = o_c + _dot_nt((q_c * jnp.exp2(bcum)).astype(BF16), sh.astype(BF16))
        blast = bcum[c - 1:c, :]
        kd_c = k_c * jnp.exp2(blast - bcum)
        sh_ref[...] = sh * jnp.exp2(blast) + _dot_tn(c_i.astype(BF16), kd_c.astype(BF16)) * bd
        ms_c = _segsum(o_c * o_c, bd_bf) * (1.0 / HEAD_DIM)
        y_ref[r0:r0 + c, 512:768] = (o_c * lax.rsqrt(ms_c + RMS_EPS) * prm_ref[3:4, :]
                                     * _silu(u2_ref[r0:r0 + c, 768:1024]))
        yield
        sg = sg_ref[...]
        sg_bf = sg.astype(BF16)
        v_new = u_sol - _dot(w_sol.astype(BF16), sg_bf)
        o_b = _dot((q * egc).astype(BF16), sg_bf) + _dot(p_mat.astype(BF16), blockdiag(v_new.astype(BF16)))
        glast = gcol[c - 1:c, :]
        kd = k * jnp.exp2(glast - gcol)
        sg_ref[...] = sg * jnp.exp2(glast) + _dot_tn(kd.astype(BF16), v_new.astype(BF16)) * bd
        ms_b = _segsum(o_b * o_b, bd_bf) * (1.0 / HEAD_DIM)
        y_ref[r0:r0 + c, 256:512] = (o_b * lax.rsqrt(ms_b + RMS_EPS) * prm_ref[2:3, :]
                                     * _silu(u0_ref[r0:r0 + c, C_BZ:C_BZ + 256]))

    stages = [chunk(s) for s in range(n_sub)]
    live = True
    while live:
        live = False
        for gen in stages:
            if next(gen, "done") != "done":
                live = True
    xpb_ref[0:HIST, :] = xpb_ref[rows:rows + HIST, :]

    @pl.when(ci == nc - 1)
    def _():
        sg_out_ref[0] = sg_ref[...]
        sh_out_ref[0] = sh_ref[...]


def _recur(u, l_valid, hist_a, hist_b, sg0, sh0, wa, wb, prm):
    bsz = sg0.shape[0]
    n = u.shape[0]
    n_sub = 1
    while n_sub < 4 and (n // bsz) % (2 * n_sub * CHUNK) == 0:
        n_sub *= 2
    rows = n_sub * CHUNK
    nc = n // bsz // rows
    consts = _recur_consts()
    cnames = ("lincl", "ucat", "incl", "strict", "eye", "bd")
    cvals = [jnp.asarray(consts[k]) for k in cnames]
    row_map = lambda col: (lambda b, ci: (b * nc + ci, col))
    full2 = lambda b, ci: (0, 0)
    per_b = lambda b, ci: (b, 0, 0)
    in_specs = [
        pl.BlockSpec((rows, 1024), row_map(0)),
        pl.BlockSpec((rows, 1024), row_map(1)),
        pl.BlockSpec((rows, 1024), row_map(2)),
        pl.BlockSpec((rows, 256), row_map(C_BB // 256)),
        pl.BlockSpec((1, HIST, 256), per_b),
        pl.BlockSpec((1, HIST, 768), per_b),
        pl.BlockSpec((1, 256, 256), per_b),
        pl.BlockSpec((1, 256, 256), per_b),
        pl.BlockSpec(wa.shape, full2),
        pl.BlockSpec(wb.shape, full2),
        pl.BlockSpec(prm.shape, full2),
    ] + [pl.BlockSpec(cv.shape, full2) for cv in cvals]
    out_shape = (jax.ShapeDtypeStruct((n, 768), F32),
                 jax.ShapeDtypeStruct((bsz, 256, 256), F32),
                 jax.ShapeDtypeStruct((bsz, 256, 256), F32),
                 jax.ShapeDtypeStruct((bsz, HIST, 256), F32))
    out_specs = (pl.BlockSpec((rows, 768), lambda b, ci: (b * nc + ci, 0)),
                 pl.BlockSpec((1, 256, 256), per_b),
                 pl.BlockSpec((1, 256, 256), per_b),
                 pl.BlockSpec((1, HIST, 256), per_b))
    scratch = [pltpu.VMEM((256, 256), F32), pltpu.VMEM((256, 256), F32),
               pltpu.VMEM((rows + HIST, 256), F32), pltpu.VMEM((rows + HIST, 768), F32),
               pltpu.VMEM((n_sub, CHUNK, 256), F32), pltpu.VMEM((n_sub, CHUNK, 256), F32),
               pltpu.VMEM((n_sub, SUB * CHUNK, 256), BF16)]
    assert n_sub == 1 or l_valid == CHUNK
    return pl.pallas_call(
        functools.partial(_recur_kernel, l_valid, n_sub),
        out_shape=out_shape,
        grid=(bsz, nc),
        in_specs=in_specs,
        out_specs=out_specs,
        scratch_shapes=scratch,
        compiler_params=pltpu.CompilerParams(
            dimension_semantics=("parallel", "arbitrary"), vmem_limit_bytes=VMEM_LIMIT),
        name="recur",
    )(u, u, u, u, hist_a, hist_b, sg0, sh0, wa, wb, prm, *cvals)


def _t5_bucket_np(dist):
    n = np.maximum(dist, 0)
    nf = np.maximum(n, MAX_EXACT).astype(np.float32)
    ratio = (np.log(nf / np.float32(MAX_EXACT)) / np.float32(math.log(MAX_DISTANCE / MAX_EXACT))).astype(np.float32)
    large = MAX_EXACT + (ratio * np.float32(N_BUCKETS - MAX_EXACT)).astype(np.int32)
    large = np.minimum(large, N_BUCKETS - 1)
    return np.where(n < MAX_EXACT, n, large).astype(np.int32)


def _bias_from_buckets(bkt, relb_ref, h):
    out = jnp.full(bkt.shape, NEG, F32)
    for b in range(N_BUCKETS):
        out = jnp.where(bkt == b, relb_ref[b, h] * LOG2E, out)
    return out


def _head_lane_mask(h, shape):
    lane = lax.broadcasted_iota(jnp.int32, shape, len(shape) - 1)
    return (lane >= h * HEAD_DIM) & (lane < (h + 1) * HEAD_DIM)


def _map_lane_mask(h, m, shape):
    lane = lax.broadcasted_iota(jnp.int32, shape, len(shape) - 1)
    lo = h * HEAD_DIM + m * HEAD_DIM_DQK
    return (lane >= lo) & (lane < lo + HEAD_DIM_DQK)


def _attn_finish(o1, o2, lam_ref, nd_ref, bd_bf):
    od = o1 - lam_ref[...] * o2
    ms = _segsum(od * od, bd_bf) * (1.0 / HEAD_DIM)
    return od * lax.rsqrt(ms + RMS_EPS) * nd_ref[...]


def _attn_prompt_kernel(qi_ref, kj_ref, relb_ref, q_ref, k_ref, v_ref, bkt_ref, lam_ref, nd_ref, bd_ref, o_ref,
                        q8_ref, m_ref, l_ref, acc_ref, bias_ref, s_ref, p_ref, al_ref):
    b = pl.program_id(0)
    step = pl.program_id(1)
    i = qi_ref[step]
    j = kj_ref[step]

    @pl.when((b == 0) & (step == 0))
    def _():
        for d in range(3):
            bkt = bkt_ref[d]
            for h in range(N_HEADS):
                bias_ref[d, h] = _bias_from_buckets(bkt, relb_ref, h)

    @pl.when(j == 0)
    def _():
        qt = (q_ref[...] * QK_SCALE_LOG2).T
        for hm in range(2 * N_HEADS):
            q8_ref[hm] = qt[hm * HEAD_DIM_DQK:(hm + 1) * HEAD_DIM_DQK, :].astype(BF16)
        m_ref[...] = jnp.full(m_ref.shape, NEG, F32)
        l_ref[...] = jnp.zeros(l_ref.shape, F32)
        acc_ref[...] = jnp.zeros(acc_ref.shape, F32)

    def update():
        vt = v_ref[...].T.astype(BF16)
        d = jnp.minimum(i - j, 2)
        for hm in range(2 * N_HEADS):
            k_hm = k_ref[:, hm * HEAD_DIM_DQK:(hm + 1) * HEAD_DIM_DQK].astype(BF16)
            s_ref[hm] = _dot(k_hm, q8_ref[hm]) + bias_ref[d, hm // 2]
        for hm in range(2 * N_HEADS):
            s = s_ref[hm]
            m_old = m_ref[hm]
            m_new = jnp.maximum(m_old, jnp.max(s, axis=0, keepdims=True))
            p = jnp.exp2(s - m_new)
            alpha = jnp.exp2(m_old - m_new)
            l_ref[hm] = alpha * l_ref[hm] + jnp.sum(p, axis=0, keepdims=True)
            p_ref[hm] = p.astype(BF16)
            al_ref[hm] = alpha
            m_ref[hm] = m_new
        for hm in range(2 * N_HEADS):
            h = hm // 2
            acc_ref[hm] = al_ref[hm] * acc_ref[hm] + _dot(vt[h * HEAD_DIM:(h + 1) * HEAD_DIM, :], p_ref[hm])

    update()

    @pl.when(j == i)
    def _():
        o1 = jnp.concatenate([acc_ref[2 * h] / l_ref[2 * h] for h in range(N_HEADS)], axis=0)
        o2 = jnp.concatenate([acc_ref[2 * h + 1] / l_ref[2 * h + 1] for h in range(N_HEADS)], axis=0)
        o_ref[...] = _attn_finish(o1.T, o2.T, lam_ref, nd_ref, bd_ref[...].astype(BF16))


def _attn_prompt(u, bsz, seq, rel_bias, lam_row, nd_row, bd):
    t = min(256, seq)
    nq = seq // t
    dist = [d * t + np.arange(t)[None, :] - np.arange(t)[:, None] for d in range(3)]
    bkt = np.stack([np.where(dd >= 0, _t5_bucket_np(dd), -1) for dd in dist]).astype(np.int32)
    assert nq <= 2 or t >= MAX_DISTANCE, "blocks two or more tiles back must share the last bucket"
    pairs = [(i, j) for i in range(nq) for j in range(i + 1)]
    qi = jnp.asarray(np.array([p[0] for p in pairs], np.int32))
    kj = jnp.asarray(np.array([p[1] for p in pairs], np.int32))
    full2 = lambda b, s, qi, kj: (0, 0)
    grid_spec = pltpu.PrefetchScalarGridSpec(
        num_scalar_prefetch=2,
        grid=(bsz, len(pairs)),
        in_specs=[
            pl.BlockSpec(memory_space=pltpu.SMEM),
            pl.BlockSpec((t, GROUP_W), lambda b, s, qi, kj: (b * nq + qi[s], C_DQ // 256)),
            pl.BlockSpec((t, GROUP_W), lambda b, s, qi, kj: (b * nq + kj[s], C_DK // 256)),
            pl.BlockSpec((t, GROUP_W), lambda b, s, qi, kj: (b * nq + kj[s], C_DV // 256)),
            pl.BlockSpec((3, t, t), lambda b, s, qi, kj: (0, 0, 0)),
            pl.BlockSpec((1, GROUP_W), full2),
            pl.BlockSpec((1, GROUP_W), full2),
            pl.BlockSpec((GROUP_W, GROUP_W), full2),
        ],
        out_specs=pl.BlockSpec((t, GROUP_W), lambda b, s, qi, kj: (b * nq + qi[s], 0)),
        scratch_shapes=[pltpu.VMEM((8, HEAD_DIM_DQK, t), BF16), pltpu.VMEM((8, 1, t), F32),
                        pltpu.VMEM((8, 1, t), F32), pltpu.VMEM((8, HEAD_DIM, t), F32),
                        pltpu.VMEM((3, N_HEADS, t, t), F32), pltpu.VMEM((8, t, t), F32),
                        pltpu.VMEM((8, t, t), BF16), pltpu.VMEM((8, 1, t), F32)],
    )
    return pl.pallas_call(
        _attn_prompt_kernel,
        out_shape=jax.ShapeDtypeStruct((bsz * seq, GROUP_W), F32),
        grid_spec=grid_spec,
        compiler_params=pltpu.CompilerParams(
            dimension_semantics=("arbitrary", "arbitrary"), vmem_limit_bytes=VMEM_LIMIT),
        name="attn_prompt",
    )(qi, kj, rel_bias, u, u, u, jnp.asarray(bkt), lam_row, nd_row, bd)


def _attn_sample_kernel(pp, n_steps, pt_ref, relb_ref, q_ref, kn_ref, vn_ref, *rest):
    k_refs = rest[:pp]
    v_refs = rest[pp:2 * pp]
    (bktl_ref, bktn_ref, lam_ref, nd_ref, o_ref,
     q16_ref, m_ref, l_ref, acc_ref, bias_ref, biasn_ref) = rest[2 * pp:]
    b = pl.program_id(0)
    s_id = pl.program_id(1)
    rows = 8
    page = k_refs[0].shape[-1]

    @pl.when((b == 0) & (s_id == 0))
    def _():
        for h in range(N_HEADS):
            near = _bias_from_buckets(bktl_ref[...], relb_ref, h)
            far = jnp.full((rows, page), relb_ref[N_BUCKETS - 1, h] * LOG2E, F32)
            newb = _bias_from_buckets(bktn_ref[...], relb_ref, h)
            for m in range(2):
                r0 = (2 * h + m) * rows
                for t in range(pp):
                    bias_ref[0, r0:r0 + rows, t * page:(t + 1) * page] = far
                    bias_ref[1, r0:r0 + rows, t * page:(t + 1) * page] = near if t == pp - 1 else far
                biasn_ref[r0:r0 + rows, :] = newb

    hrows = 2 * rows

    @pl.when(s_id == 0)
    def _():
        lane = lax.broadcasted_iota(jnp.int32, (rows, HEAD_DIM), 1)
        for h in range(N_HEADS):
            qh = q_ref[0, h] * QK_SCALE_LOG2
            q16_ref[h * hrows:h * hrows + rows, :] = jnp.where(lane < HEAD_DIM_DQK, qh, 0.0)
            q16_ref[h * hrows + rows:(h + 1) * hrows, :] = jnp.where(lane >= HEAD_DIM_DQK, qh, 0.0)
        s = jnp.concatenate(
            [_dot_nt(q16_ref[h * hrows:(h + 1) * hrows, :].astype(BF16), kn_ref[0, h].astype(BF16))
             for h in range(N_HEADS)], axis=0) + biasn_ref[...]
        m0 = jnp.max(s, axis=1, keepdims=True)
        p = jnp.exp2(s - m0)
        m_ref[...] = m0
        l_ref[...] = jnp.sum(p, axis=1, keepdims=True)
        for h in range(N_HEADS):
            acc_ref[h * hrows:(h + 1) * hrows, :] = _dot(
                p[h * hrows:(h + 1) * hrows, :].astype(BF16), vn_ref[0, h].astype(BF16))

    sel = jnp.where(s_id == n_steps - 1, 1, 0)
    s = jnp.concatenate(
        [_dot(q16_ref[h * hrows:(h + 1) * hrows, :].astype(BF16),
              jnp.concatenate([r[0, 0, h] for r in k_refs], axis=1).astype(BF16))
         for h in range(N_HEADS)], axis=0) + bias_ref[sel]
    m_old = m_ref[...]
    m_new = jnp.maximum(m_old, jnp.max(s, axis=1, keepdims=True))
    p32 = jnp.exp2(s - m_new)
    alpha = jnp.exp2(m_old - m_new)
    l_ref[...] = alpha * l_ref[...] + jnp.sum(p32, axis=1, keepdims=True)
    p = p32.astype(BF16)
    pv = jnp.concatenate(
        [_dot_nt(p[h * hrows:(h + 1) * hrows, :],
                 jnp.concatenate([r[0, 0, h] for r in v_refs], axis=1).astype(BF16))
         for h in range(N_HEADS)], axis=0)
    acc_ref[...] = alpha * acc_ref[...] + pv
    m_ref[...] = m_new

    @pl.when(s_id == n_steps - 1)
    def _():
        o = acc_ref[...] / l_ref[...]
        for h in range(N_HEADS):
            od = o[h * hrows:h * hrows + rows, :] - lam_ref[...] * o[h * hrows + rows:(h + 1) * hrows, :]
            ms = jnp.mean(od * od, axis=1, keepdims=True)
            o_ref[0, h] = od * lax.rsqrt(ms + RMS_EPS) * nd_ref[...]


def _attn_sample(layer, q4, k4, v4, ck, cv, page_table, l_new, rel_bias, lam_row, nd_row):
    bsz = q4.shape[0]
    n_pages = page_table.shape[1]
    page = ck.shape[-1]
    pp = 16
    while n_pages % pp:
        pp //= 2
    n_steps = n_pages // pp
    past = n_pages * page
    qpos = past + np.arange(8)[:, None]
    bkt_last = _t5_bucket_np(qpos - (past - page + np.arange(page))[None, :])
    dist_new = np.arange(8)[:, None] - np.arange(8)[None, :]
    ok = (dist_new >= 0) & (np.arange(8)[None, :] < l_new)
    bkt_new = np.where(ok, _t5_bucket_np(dist_new), -1).astype(np.int32)
    assert page >= MAX_DISTANCE, "only the last cache page may need distance-dependent bias"

    def page_map(t):
        return lambda b, s, pt: (layer, pt[b, s * pp + t], 0, 0, 0)

    per_b = lambda b, s, pt: (b, 0, 0, 0)
    full2 = lambda b, s, pt: (0, 0)
    new_spec = pl.BlockSpec((1, N_HEADS, 8, HEAD_DIM), per_b)
    page_spec = lambda t: pl.BlockSpec((1, 1, N_HEADS, HEAD_DIM, page), page_map(t))
    in_specs = ([pl.BlockSpec(memory_space=pltpu.SMEM), new_spec, new_spec, new_spec]
                + [page_spec(t) for t in range(pp)]
                + [page_spec(t) for t in range(pp)]
                + [pl.BlockSpec((8, page), full2), pl.BlockSpec((8, 8), full2),
                   pl.BlockSpec((1, HEAD_DIM), full2), pl.BlockSpec((1, HEAD_DIM), full2)])
    grid_spec = pltpu.PrefetchScalarGridSpec(
        num_scalar_prefetch=1,
        grid=(bsz, n_steps),
        in_specs=in_specs,
        out_specs=new_spec,
        scratch_shapes=[pltpu.VMEM((N_HEADS * 16, HEAD_DIM), F32), pltpu.VMEM((N_HEADS * 16, 1), F32),
                        pltpu.VMEM((N_HEADS * 16, 1), F32), pltpu.VMEM((N_HEADS * 16, HEAD_DIM), F32),
                        pltpu.VMEM((2, N_HEADS * 16, pp * page), F32), pltpu.VMEM((N_HEADS * 16, 8), F32)],
    )
    return pl.pallas_call(
        functools.partial(_attn_sample_kernel, pp, n_steps),
        out_shape=jax.ShapeDtypeStruct((bsz, N_HEADS, 8, HEAD_DIM), F32),
        grid_spec=grid_spec,
        compiler_params=pltpu.CompilerParams(
            dimension_semantics=("arbitrary", "arbitrary"), vmem_limit_bytes=VMEM_LIMIT),
        name="attn_sample",
    )(page_table, rel_bias, q4, k4, v4, *([ck] * pp), *([cv] * pp),
      jnp.asarray(bkt_last), jnp.asarray(bkt_new), lam_row[:, :HEAD_DIM], nd_row[:, :HEAD_DIM])


def _oproj_kernel(alpha, yabc_ref, yd_ref, x_ref, wo_ref, g_ref, b_ref, o_ref):
    mix = (_dot(yabc_ref[...].astype(BF16), wo_ref[0:768, :])
           + _dot(yd_ref[...].astype(BF16), wo_ref[768:1024, :]))
    o_ref[...] = _layer_norm(alpha * x_ref[...] + mix, g_ref[...], b_ref[...])


def _oproj(yabc, yd, x, wo, g, b, alpha, tm):
    n = x.shape[0]
    row = lambda i: (i, 0)
    full = lambda i: (0, 0)
    return pl.pallas_call(
        functools.partial(_oproj_kernel, alpha),
        out_shape=jax.ShapeDtypeStruct((n, D_MODEL), F32),
        grid=(n // tm,),
        in_specs=[pl.BlockSpec((tm, 768), row), pl.BlockSpec((tm, 256), row),
                  pl.BlockSpec((tm, D_MODEL), row), pl.BlockSpec((D_MODEL, D_MODEL), full),
                  pl.BlockSpec((1, D_MODEL), full), pl.BlockSpec((1, D_MODEL), full)],
        out_specs=pl.BlockSpec((tm, D_MODEL), row),
        compiler_params=pltpu.CompilerParams(
            dimension_semantics=("parallel",), vmem_limit_bytes=VMEM_LIMIT),
        name="out_proj_ln",
    )(yabc, yd, x, wo, g, b)


def _ffn_kernel(alpha, x_ref, wg_ref, wu_ref, wd_ref, g_ref, b_ref, o_ref, xb_ref, acc_ref):
    f = pl.program_id(1)

    @pl.when(f == 0)
    def _():
        xb_ref[...] = x_ref[...].astype(BF16)
        acc_ref[...] = jnp.zeros(acc_ref.shape, F32)

    xb = xb_ref[...]
    h = _silu(_dot(xb, wg_ref[...])) * _dot(xb, wu_ref[...])
    acc_ref[...] += _dot(h.astype(BF16), wd_ref[...])

    @pl.when(f == pl.num_programs(1) - 1)
    def _():
        o_ref[...] = _layer_norm(alpha * x_ref[...] + acc_ref[...], g_ref[...], b_ref[...])


def _ffn(x, wg, wu, wd, g, b, alpha, tm, tf):
    n = x.shape[0]
    ff = wg.shape[1]
    row = lambda i, f: (i, 0)
    full = lambda i, f: (0, 0)
    return pl.pallas_call(
        functools.partial(_ffn_kernel, alpha),
        out_shape=jax.ShapeDtypeStruct((n, D_MODEL), F32),
        grid=(n // tm, ff // tf),
        in_specs=[pl.BlockSpec((tm, D_MODEL), row),
                  pl.BlockSpec((D_MODEL, tf), lambda i, f: (0, f)),
                  pl.BlockSpec((D_MODEL, tf), lambda i, f: (0, f)),
                  pl.BlockSpec((tf, D_MODEL), lambda i, f: (f, 0)),
                  pl.BlockSpec((1, D_MODEL), full), pl.BlockSpec((1, D_MODEL), full)],
        out_specs=pl.BlockSpec((tm, D_MODEL), row),
        scratch_shapes=[pltpu.VMEM((tm, D_MODEL), BF16), pltpu.VMEM((tm, D_MODEL), F32)],
        compiler_params=pltpu.CompilerParams(
            dimension_semantics=("parallel", "arbitrary"), vmem_limit_bytes=VMEM_LIMIT),
        name="ffn_ln",
    )(x, wg, wu, wd, g, b)


def _moe_kernel(alpha, rb, x_ref, wrt_ref, wg_ref, wu_ref, wd_ref, g_ref, b_ref, o_ref,
                xb_ref, sel_ref, rank_ref, gate_ref, upper_ref, gall_ref, yh_ref, yl_ref):
    e = pl.program_id(1)
    tile = x_ref.shape[0]

    @pl.when(e == 0)
    def _():
        x = x_ref[...]
        xb_ref[...] = x.astype(BF16)
        o_ref[...] = jnp.zeros(o_ref.shape, F32)
        r_i = lax.broadcasted_iota(jnp.int32, (tile, tile), 0)
        c_i = lax.broadcasted_iota(jnp.int32, (tile, tile), 1)
        upper_ref[...] = jnp.where(r_i < c_i, 1.0, 0.0).astype(BF16)
        logits = lax.dot_general(wrt_ref[...], x, (((1,), (1,)), ((), ())),
                                 preferred_element_type=F32, precision=lax.Precision.HIGHEST)
        eid = lax.broadcasted_iota(jnp.int32, logits.shape, 0)
        logits = jnp.where(eid < N_EXPERTS, logits, NEG)
        big = logits.shape[0]
        m1 = jnp.max(logits, axis=0, keepdims=True)
        i1 = jnp.min(jnp.where(logits == m1, eid, big), axis=0, keepdims=True)
        rest = jnp.where(eid == i1, NEG, logits)
        m2 = jnp.max(rest, axis=0, keepdims=True)
        i2 = jnp.min(jnp.where(rest == m2, eid, big), axis=0, keepdims=True)
        e2 = jnp.exp(m2 - m1)
        g1 = 1.0 / (1.0 + e2)
        g2 = e2 / (1.0 + e2)
        sel = jnp.where((eid == i1) | (eid == i2), 1.0, 0.0)
        sel_ref[...] = sel
        gate_ref[...] = jnp.where(eid == i1, g1, 0.0) + jnp.where(eid == i2, g2, 0.0)
        rank_ref[...] = _dot(sel.astype(BF16), upper_ref[...])

    sel_row = sel_ref[pl.ds(e, 1), :]
    rank_row = rank_ref[pl.ds(e, 1), :]
    gate_row = gate_ref[pl.ds(e, 1), :]
    count = jnp.sum(sel_row).astype(jnp.int32)
    slot = lax.broadcasted_iota(jnp.int32, (rb, tile), 0).astype(F32)

    def expert_block(base):
        onehot = (slot + base == rank_row) & (sel_row > 0.5)
        gather = jnp.where(onehot, 1.0, 0.0).astype(BF16)
        xg = _dot(gather, xb_ref[...]).astype(BF16)
        h = _silu(_dot(xg, wg_ref[0])) * _dot(xg, wu_ref[0])
        y = _dot(h.astype(BF16), wd_ref[0])
        y = y * jnp.sum(jnp.where(onehot, gate_row, 0.0), axis=1, keepdims=True)
        return (gather,) + _split2(y)

    rows0 = pl.ds(pl.multiple_of(e * rb, 16), rb)
    gall_ref[rows0, :], yh_ref[rows0, :], yl_ref[rows0, :] = expert_block(0.0)

    def overflow(blk, carry):
        gather, yh, yl = expert_block((blk * rb).astype(F32))
        o_ref[...] += _dot_tn(gather, yh) + _dot_tn(gather, yl)
        return carry

    lax.fori_loop(1, (count + rb - 1) // rb, overflow, 0)

    @pl.when(e == pl.num_programs(1) - 1)
    def _():
        mix = o_ref[...] + _dot_tn(gall_ref[...], yh_ref[...]) + _dot_tn(gall_ref[...], yl_ref[...])
        o_ref[...] = _layer_norm(alpha * x_ref[...] + mix, g_ref[...], b_ref[...])


def _moe(x, wrt, wg, wu, wd, g, b, alpha, tile, rb):
    n = x.shape[0]
    n_e, _, ff = wg.shape
    row = lambda i, e: (i, 0)
    full = lambda i, e: (0, 0)
    per_e = lambda i, e: (e, 0, 0)
    return pl.pallas_call(
        functools.partial(_moe_kernel, alpha, rb),
        out_shape=jax.ShapeDtypeStruct((n, D_MODEL), F32),
        grid=(n // tile, n_e),
        in_specs=[pl.BlockSpec((tile, D_MODEL), row, pipeline_mode=pl.Buffered(1)),
                  pl.BlockSpec(wrt.shape, full),
                  pl.BlockSpec((1, D_MODEL, ff), per_e),
                  pl.BlockSpec((1, D_MODEL, ff), per_e),
                  pl.BlockSpec((1, ff, D_MODEL), per_e),
                  pl.BlockSpec((1, D_MODEL), full), pl.BlockSpec((1, D_MODEL), full)],
        out_specs=pl.BlockSpec((tile, D_MODEL), row),
        scratch_shapes=[pltpu.VMEM((tile, D_MODEL), BF16), pltpu.VMEM((16, tile), F32),
                        pltpu.VMEM((16, tile), F32), pltpu.VMEM((16, tile), F32),
                        pltpu.VMEM((tile, tile), BF16), pltpu.VMEM((n_e * rb, tile), BF16),
                        pltpu.VMEM((n_e * rb, D_MODEL), BF16), pltpu.VMEM((n_e * rb, D_MODEL), BF16)],
        compiler_params=pltpu.CompilerParams(
            dimension_semantics=("parallel", "arbitrary"), vmem_limit_bytes=MOE_VMEM_LIMIT),
        name="moe_ln",
    )(x, wrt, wg, wu, wd, g, b)


def _w_in_columns():
    o = {}
    off = 0
    for name, size in (("a_in", 256), ("a_gb", 256), ("a_gc", 256), ("b_qkv", 768), ("b_a", 4), ("b_b", 4),
                       ("b_z", 256), ("c_q", 256), ("c_f", 256), ("c_i", 256), ("c_g", 256),
                       ("d_q", 256), ("d_k", 256), ("d_v", 256)):
        o[name] = np.arange(off, off + size)
        off += size
    rep = lambda ix: np.repeat(ix, HEAD_DIM)
    cols = np.concatenate([o["a_in"], o["a_gb"], o["a_gc"], o["b_z"],
                           o["b_qkv"], rep(o["b_a"]),
                           o["c_q"], o["c_f"], o["c_i"], o["c_g"],
                           rep(o["b_b"]), o["d_q"], o["d_k"], o["d_v"]])
    assert cols.shape[0] == U_W
    return cols


def _pick_tile(n, pref):
    t = min(pref, n)
    while n % t:
        t //= 2
    return t


def _blockdiag_state(s):
    bsz = s.shape[0]
    eye = jnp.eye(N_HEADS, dtype=s.dtype)
    return (s[:, :, :, None, :] * eye[None, :, None, :, None]).reshape(bsz, GROUP_W, GROUP_W)


def _diag_blocks(s):
    bsz = s.shape[0]
    s5 = s.reshape(bsz, N_HEADS, HEAD_DIM, N_HEADS, HEAD_DIM)
    return jnp.stack([s5[:, h, :, h, :] for h in range(N_HEADS)], axis=1)


def _run_group(x, init, attend, lw, depth):
    bsz, seq, _ = x.shape
    lp = -(-seq // CHUNK) * CHUNK
    assert (lp == seq or seq < CHUNK) and seq >= CONV_B_W - 1
    n = bsz * seq
    alpha = (2 * depth) ** 0.25
    xf = x.reshape(n, D_MODEL)
    tm = _pick_tile(n, 512)
    outs = []
    for l in range(depth):
        w = lw[l]
        u = _proj(xf, w["w_in"], tm, 1024)
        u3 = u.reshape(bsz, seq, U_W)
        if init is None:
            hist_a = jnp.zeros((bsz, HIST, 256), F32)
            hist_b = jnp.zeros((bsz, HIST, 768), F32)
            sg0 = jnp.zeros((bsz, GROUP_W, GROUP_W), F32)
            sh0 = sg0
        else:
            ca, cb, s_g, s_h = init[l]
            hist_a = jnp.pad(ca, ((0, 0), (HIST - ca.shape[1], 0), (0, 0)))
            hist_b = jnp.pad(cb, ((0, 0), (HIST - cb.shape[1], 0), (0, 0)))
            sg0 = _blockdiag_state(s_g)
            sh0 = _blockdiag_state(jnp.swapaxes(s_h, 2, 3))
        u_rec = u if lp == seq else jnp.pad(u3, ((0, 0), (0, lp - seq), (0, 0))).reshape(bsz * lp, U_W)
        yabc, sg1, sh1, tail_a = _recur(u_rec, min(seq, CHUNK) if lp != seq else CHUNK,
                                        hist_a, hist_b, sg0, sh0, w["conv_a"], w["conv_b"], w["prm"])
        if lp != seq:
            yabc = yabc.reshape(bsz, lp, 768)[:, :seq].reshape(n, 768)
        yd = attend(l, u, u3, w)
        x1 = _oproj(yabc, yd, xf, w["w_o"], w["ln1_g"], w["ln1_b"], alpha, tm)
        f = w["ffn"]
        if f["moe"]:
            tile = _pick_tile(n, 1024)
            rb = -(-(tile * 9 // 32) // 16) * 16
            xf = _moe(x1, f["wrt"], f["wg"], f["wu"], f["wd"], w["ln2_g"], w["ln2_b"], alpha, tile, rb)
        else:
            xf = _ffn(x1, f["wg"], f["wu"], f["wd"], w["ln2_g"], w["ln2_b"], alpha,
                      _pick_tile(n, 512), f["tf"])
        a0 = HIST - (CONV_A_W - 1)
        outs.append((
            u3[:, :, C_DK:C_DK + 256].reshape(bsz, seq, N_HEADS, HEAD_DIM),
            u3[:, :, C_DV:C_DV + 256].reshape(bsz, seq, N_HEADS, HEAD_DIM),
            tail_a[:, a0:a0 + CONV_A_W - 1],
            u3[:, seq - (CONV_B_W - 1):, C_BQKV:C_BQKV + 768],
            _diag_blocks(sg1),
            jnp.swapaxes(_diag_blocks(sh1), 2, 3),
        ))
    return xf.reshape(bsz, seq, D_MODEL), [jnp.stack([o[i] for o in outs]) for i in range(6)]


def kernel(x_prompt, x_sample, cache_k, cache_v, state_conv_a, state_conv_b, state_gdn, state_hgrn, page_table, w_in, conv_a, conv_b, gdn_a_log, gdn_dt_bias, norm_b, lower_bounds, norm_c, lambda_q1, lambda_k1, lambda_q2, lambda_k2, norm_d, rel_bias, w_o, ln1_g, ln1_b, ffn_w_gate, ffn_w_up, ffn_w_down, router_w, moe_w_gate, moe_w_up, moe_w_down, ln2_g, ln2_b):
    depth = w_in.shape[0]
    cols = _w_in_columns()
    rep = lambda t: jnp.repeat(t.astype(F32), HEAD_DIM)
    tile4 = lambda t: jnp.tile(t.astype(F32), N_HEADS)
    lbs = jax.nn.softmax(lower_bounds.astype(F32), axis=0)
    lb_all = jnp.cumsum(lbs, axis=0) - lbs[0]
    bd = jnp.asarray(_recur_consts()["bd"])
    lw = []
    for l in range(depth):
        lam_init = 0.8 - 0.6 * math.exp(-0.3 * l)
        lam = (jnp.exp(jnp.sum(lambda_q1[l].astype(F32) * lambda_k1[l].astype(F32)))
               - jnp.exp(jnp.sum(lambda_q2[l].astype(F32) * lambda_k2[l].astype(F32))) + lam_init)
        prm = jnp.stack([-jnp.exp(rep(gdn_a_log[l])) * LOG2E, rep(gdn_dt_bias[l]), tile4(norm_b[l]),
                         tile4(norm_c[l]), lb_all[l], jnp.zeros((GROUP_W,), F32),
                         jnp.zeros((GROUP_W,), F32), jnp.zeros((GROUP_W,), F32)])
        j = l // 2
        if l % 2 == 0:
            ff = ffn_w_gate.shape[2]
            ffn = dict(moe=False, wg=ffn_w_gate[j].astype(BF16), wu=ffn_w_up[j].astype(BF16),
                       wd=ffn_w_down[j].astype(BF16))
        else:
            ff = moe_w_gate.shape[3]
            ffn = dict(moe=True, wrt=jnp.pad(router_w[j].astype(F32).T, ((0, 16 - N_EXPERTS), (0, 0))),
                       wg=moe_w_gate[j].astype(BF16), wu=moe_w_up[j].astype(BF16),
                       wd=moe_w_down[j].astype(BF16))
        tf = ff
        for cand in (1408, 1024, 512, 256, 128):
            if ff % cand == 0:
                tf = cand
                break
        ffn["tf"] = tf
        lw.append(dict(
            w_in=w_in[l][:, cols].astype(BF16),
            conv_a=jnp.pad(conv_a[l].astype(F32), ((0, HIST - CONV_A_W), (0, 0))),
            conv_b=jnp.pad(conv_b[l].astype(F32), ((0, HIST - CONV_B_W), (0, 0))),
            prm=prm,
            lam_row=jnp.full((1, GROUP_W), lam, F32),
            nd_row=(tile4(norm_d[l]) * (1.0 - lam_init))[None, :],
            w_o=w_o[l].astype(BF16),
            ln1_g=ln1_g[l][None].astype(F32), ln1_b=ln1_b[l][None].astype(F32),
            ln2_g=ln2_g[l][None].astype(F32), ln2_b=ln2_b[l][None].astype(F32),
            ffn=ffn,
        ))
    relb = rel_bias.astype(F32)

    def attend_prompt(l, u, u3, w):
        bsz, seq, _ = u3.shape
        return _attn_prompt(u, bsz, seq, relb, w["lam_row"], w["nd_row"], bd)

    ck_t = jnp.transpose(cache_k, (0, 1, 3, 4, 2))
    cv_t = jnp.transpose(cache_v, (0, 1, 3, 4, 2))

    def attend_sample(l, u, u3, w):
        bsz, seq, _ = u3.shape

        def heads8(c0):
            t = u3[:, :, c0:c0 + 256].reshape(bsz, seq, N_HEADS, HEAD_DIM)
            return jnp.pad(jnp.swapaxes(t, 1, 2), ((0, 0), (0, 0), (0, 8 - seq), (0, 0)))

        o = _attn_sample(l, heads8(C_DQ), heads8(C_DK), heads8(C_DV), ck_t, cv_t,
                         page_table, seq, relb, w["lam_row"], w["nd_row"])
        return jnp.swapaxes(o[:, :, :seq], 1, 2).reshape(bsz * seq, GROUP_W)

    init_s = [(state_conv_a[l], state_conv_b[l], state_gdn[l], state_hgrn[l]) for l in range(depth)]
    y_p, (k_p, v_p, ca_p, cb_p, sg_p, sh_p) = _run_group(x_prompt, None, attend_prompt, lw, depth)
    y_s, (k_s, v_s, ca_s, cb_s, sg_s, sh_s) = _run_group(x_sample, init_s, attend_sample, lw, depth)
    return (y_p, y_s, k_p, v_p, k_s, v_s, ca_p, ca_s, cb_p, cb_s, sg_p, sg_s, sh_p, sh_s)
```

```python
import functools
import math

import numpy as np
import jax
import jax.numpy as jnp
from jax import lax
from jax.experimental import pallas as pl
from jax.experimental.pallas import tpu as pltpu

F32 = jnp.float32
BF16 = jnp.bfloat16

D_MODEL = 1024
GROUP_W = 256
N_HEADS = 4
HEAD_DIM = 64
CHUNK = 64
HIST = 8
SUB = 16
SUB_SHIFT = 4
CONV_A_W = 3
CONV_B_W = 4
HEAD_DIM_DQK = 32
N_BUCKETS = 32
MAX_EXACT = 16
MAX_DISTANCE = 128
N_EXPERTS = 8
LN_EPS = 1e-5
RMS_EPS = 1e-6
NEG = -1e30
LOG2E = math.log2(math.e)
QK_SCALE_LOG2 = HEAD_DIM_DQK ** -0.5 * LOG2E
U_W = 4096
VMEM_LIMIT = 56 * 1024 * 1024
MOE_VMEM_LIMIT = 60 * 1024 * 1024

C_AIN, C_AGB, C_AGC, C_BZ = 0, 256, 512, 768
C_BQKV, C_BA = 1024, 1792
C_CQ, C_CF, C_CI, C_CG = 2048, 2304, 2560, 2816
C_BB, C_DQ, C_DK, C_DV = 3072, 3328, 3584, 3840


def _dot(a, b):
    return jnp.dot(a, b, preferred_element_type=F32)


def _dot_nt(a, b):
    return lax.dot_general(a, b, (((1,), (1,)), ((), ())), preferred_element_type=F32)


def _dot_tn(a, b):
    return lax.dot_general(a, b, (((0,), (0,)), ((), ())), preferred_element_type=F32)


def _split2(x):
    hi = x.astype(BF16)
    lo = (x - hi.astype(F32)).astype(BF16)
    return hi, lo


def _dot_c2(c_bf, x):
    hi, lo = _split2(x)
    return _dot(c_bf, hi) + _dot(c_bf, lo)


def _segsum(s, bd_bf):
    return _dot(s.astype(BF16), bd_bf)


def _sigmoid(x):
    return 0.5 * jnp.tanh(0.5 * x) + 0.5


def _silu(x):
    return x * _sigmoid(x)


def _softplus(x):
    return jnp.maximum(x, 0.0) + jnp.log(1.0 + jnp.exp(-jnp.abs(x)))


def _layer_norm(z, g, b):
    mu = jnp.mean(z, axis=-1, keepdims=True)
    zc = z - mu
    var = jnp.mean(zc * zc, axis=-1, keepdims=True)
    return zc * lax.rsqrt(var + LN_EPS) * g + b


def _proj_kernel(tn, x_ref, w_ref, o_ref):
    xb = x_ref[...].astype(BF16)
    for c0 in range(0, w_ref.shape[1], tn):
        o_ref[:, c0:c0 + tn] = _dot(xb, w_ref[:, c0:c0 + tn])


def _proj(x, w, tm, tn):
    n, k = x.shape
    m = w.shape[1]
    return pl.pallas_call(
        functools.partial(_proj_kernel, tn),
        out_shape=jax.ShapeDtypeStruct((n, m), F32),
        grid=(n // tm,),
        in_specs=[pl.BlockSpec((tm, k), lambda i: (i, 0)),
                  pl.BlockSpec((k, m), lambda i: (0, 0))],
        out_specs=pl.BlockSpec((tm, m), lambda i: (i, 0)),
        compiler_params=pltpu.CompilerParams(
            dimension_semantics=("parallel",), vmem_limit_bytes=VMEM_LIMIT),
        name="proj_in",
    )(x, w)


def _recur_consts():
    c = CHUNK
    w = GROUP_W
    r = np.arange(c)[:, None]
    lane = np.arange(w)[None, :]
    j = lane % c
    bd = (np.arange(w)[:, None] // c == lane // c)
    return dict(
        lincl=(np.arange(c)[None, :] <= r).astype(np.float32),
        ucat=(r <= j).astype(np.float32),
        incl=(j <= r).astype(np.float32),
        strict=(j < r).astype(np.float32),
        eye=(j == r).astype(np.float32),
        bd=bd.astype(np.float32),
    )


def _recur_kernel(l_valid, n_sub, u0_ref, u1_ref, u2_ref, ubb_ref, hista_ref, histb_ref, sg0_ref, sh0_ref,
                  wa_ref, wb_ref, prm_ref, lincl_ref, ucat_ref, incl_ref, strict_ref, eye_ref, bd_ref,
                  y_ref, sg_out_ref, sh_out_ref, hista_out_ref,
                  sg_ref, sh_ref, xpa_ref, xpb_ref, hbp_ref, hkp_ref, gs_ref):
    c = CHUNK
    rows = n_sub * c
    ci = pl.program_id(1)
    nc = pl.num_programs(1)

    @pl.when(ci == 0)
    def _():
        sg_ref[...] = sg0_ref[0]
        sh_ref[...] = sh0_ref[0]
        xpa_ref[0:HIST, :] = hista_ref[0]
        xpb_ref[0:HIST, :] = histb_ref[0]

    bd = bd_ref[...]
    bd_bf = bd.astype(BF16)
    lincl_bf = lincl_ref[...].astype(BF16)
    ones_bf = jnp.ones((c, c), BF16)
    incl = incl_ref[...] > 0.5
    strict = strict_ref[...] > 0.5
    eye = eye_ref[...]
    row = lax.broadcasted_iota(jnp.int32, (c, GROUP_W), 0)
    jcol = lax.broadcasted_iota(jnp.int32, (c, GROUP_W), 1) & (c - 1)
    sub = row >> SUB_SHIFT
    row_in_sub = row & (SUB - 1)
    same_sub = (jcol >> SUB_SHIFT) == sub
    jcol_in_sub = jcol & (SUB - 1)

    def blockdiag(x):
        return jnp.concatenate([x, x, x, x], axis=0) * (bd_bf if x.dtype == BF16 else bd)

    def mmc(x, y):
        return _dot(x.astype(BF16), blockdiag(y.astype(BF16)))

    def mask_rows(x):
        return x if l_valid >= c else jnp.where(row < l_valid, x, 0.0)

    xpa_ref[HIST:HIST + rows, :] = u0_ref[:, C_AGC:C_AGC + 256] * u0_ref[:, C_AIN:C_AIN + 256]
    conv_a = jnp.zeros((rows, GROUP_W), F32)
    for jj in range(CONV_A_W):
        off = HIST - (CONV_A_W - 1) + jj
        conv_a = conv_a + xpa_ref[off:off + rows, :] * wa_ref[jj:jj + 1, :]
    y_ref[:, 0:256] = u0_ref[:, C_AGB:C_AGB + 256] * conv_a
    last = l_valid if n_sub == 1 else rows
    hista_out_ref[0] = xpa_ref[last:last + HIST, :]
    xpa_ref[0:HIST, :] = xpa_ref[rows:rows + HIST, :]
    xpb_ref[HIST:HIST + rows, :] = u1_ref[:, 0:768]

    def chunk(s):
        r0 = s * c
        conv_b = jnp.zeros((c, 768), F32)
        for jj in range(CONV_B_W):
            off = HIST - (CONV_B_W - 1) + jj + r0
            conv_b = conv_b + xpb_ref[off:off + c, :] * wb_ref[jj:jj + 1, :]
        conv_b = _silu(conv_b)
        q_raw = conv_b[:, 0:256]
        k_raw = conv_b[:, 256:512]
        v = conv_b[:, 512:768]
        beta = mask_rows(_sigmoid(ubb_ref[r0:r0 + c, :]))
        g = mask_rows(prm_ref[0:1, :] * _softplus(u1_ref[r0:r0 + c, 768:1024] + prm_ref[1:2, :]))
        lb = prm_ref[4:5, :]
        sig = _sigmoid(u2_ref[r0:r0 + c, 256:512])
        log_f = mask_rows(jnp.log2(lb + (1.0 - lb) * sig))
        k_c = mask_rows((1.0 - lb) * (1.0 - sig))
        q_c = _silu(u2_ref[r0:r0 + c, 0:256])
        c_i = u2_ref[r0:r0 + c, 512:768]
        yield
        q = q_raw * lax.rsqrt(_segsum(q_raw * q_raw, bd_bf) + RMS_EPS) * (HEAD_DIM ** -0.5)
        k = k_raw * lax.rsqrt(_segsum(k_raw * k_raw, bd_bf) + RMS_EPS)
        gcol = _dot_c2(lincl_bf, g)
        grow = _dot_c2(ones_bf, g * ucat_ref[...])
        bcum = _dot_c2(lincl_bf, log_f)
        yield
        decay = jnp.exp2(jnp.where(incl, gcol - grow, NEG))
        k_bf = k.astype(BF16)
        kb_bf = blockdiag(k_bf)
        kk = _dot_nt(k_bf, kb_bf)
        qk = _dot_nt(q.astype(BF16), kb_bf)
        a_mat = jnp.where(strict, kk * decay * beta, 0.0)
        p_mat = qk * decay
        yield
        hbp_ref[s] = bcum
        hkp_ref[s] = k_c
        anchor = jnp.zeros((c, GROUP_W), F32)
        for t in range(1, c // SUB):
            anchor = jnp.where(sub == t, hbp_ref[s, t * SUB - 1:t * SUB, :], anchor)
        q_anch = (q_c * jnp.exp2(bcum - anchor)).astype(BF16)
        att = jnp.zeros((c, GROUP_W), F32)
        for t in range(1, c // SUB):
            a_t = hbp_ref[s, t * SUB - 1:t * SUB, :]
            k_anch = k_c * jnp.exp2(jnp.where(row < t * SUB, a_t - bcum, NEG))
            att = att + jnp.where(sub == t, _dot_nt(q_anch, blockdiag(k_anch.astype(BF16))), 0.0)
        yield
        t_mat = eye - a_mat
        pw = mmc(a_mat, a_mat)
        t_mat = t_mat + mmc(pw, t_mat)
        yield
        def key_rows(ref, jj):
            return jnp.concatenate(
                [jnp.broadcast_to(ref[s, t * SUB + jj:t * SUB + jj + 1, :], (SUB, GROUP_W))
                 for t in range(c // SUB)], axis=0)

        for jj in range(SUB):
            e = jnp.exp2(jnp.where(row_in_sub >= jj, bcum - key_rows(hbp_ref, jj), NEG))
            gs_ref[s, jj * c:(jj + 1) * c, :] = (q_c * e * key_rows(hkp_ref, jj)).astype(BF16)
        attb = _dot(gs_ref[s], bd_bf)
        yield
        pw = mmc(pw, pw)
        t_mat = t_mat + mmc(pw, t_mat)
        yield
        for jj in range(SUB):
            att = att + jnp.where(same_sub & (jcol_in_sub == jj), attb[jj * c:(jj + 1) * c, :], 0.0)
        o_c = _dot(att.astype(BF16), blockdiag(c_i.astype(BF16)))
        yield
        pw = mmc(pw, pw)
        t_mat = t_mat + mmc(pw, t_mat)
        yield
        pw = mmc(pw, pw)
        t_mat = t_mat + mmc(pw, t_mat)
        yield
        pw = mmc(pw, pw)
        t_mat = t_mat + mmc(pw, t_mat)
        egc = jnp.exp2(gcol)
        rhs = jnp.concatenate([blockdiag((v * beta).astype(BF16)),
                               blockdiag((k * beta * egc).astype(BF16))], axis=1)
        sol = _dot(t_mat.astype(BF16), rhs)
        u_sol = sol[:, 0:256]
        w_sol = sol[:, 256:512]
        yield
        sh = sh_ref[...]
        o_c = o_c + _dot_nt((q_c * jnp.exp2(bcum)).astype(BF16), sh.astype(BF16))
        blast = bcum[c - 1:c, :]
        kd_c = k_c * jnp.exp2(blast - bcum)
        sh_ref[...] = sh * jnp.exp2(blast) + _dot_tn(c_i.astype(BF16), kd_c.astype(BF16)) * bd
        ms_c = _segsum(o_c * o_c, bd_bf) * (1.0 / HEAD_DIM)
        y_ref[r0:r0 + c, 512:768] = (o_c * lax.rsqrt(ms_c + RMS_EPS) * prm_ref[3:4, :]
                                     * _silu(u2_ref[r0:r0 + c, 768:1024]))
        yield
        sg = sg_ref[...]
        sg_bf = sg.astype(BF16)
        v_new = u_sol - _dot(w_sol.astype(BF16), sg_bf)
        o_b = _dot((q * egc).astype(BF16), sg_bf) + _dot(p_mat.astype(BF16), blockdiag(v_new.astype(BF16)))
        glast = gcol[c - 1:c, :]
        kd = k * jnp.exp2(glast - gcol)
        sg_ref[...] = sg * jnp.exp2(glast) + _dot_tn(kd.astype(BF16), v_new.astype(BF16)) * bd
        ms_b = _segsum(o_b * o_b, bd_bf) * (1.0 / HEAD_DIM)
        y_ref[r0:r0 + c, 256:512] = (o_b * lax.rsqrt(ms_b + RMS_EPS) * prm_ref[2:3, :]
                                     * _silu(u0_ref[r0:r0 + c, C_BZ:C_BZ + 256]))

    stages = [chunk(s) for s in range(n_sub)]
    live = True
    while live:
        live = False
        for gen in stages:
            if next(gen, "done") != "done":
                live = True
    xpb_ref[0:HIST, :] = xpb_ref[rows:rows + HIST, :]

    @pl.when(ci == nc - 1)
    def _():
        sg_out_ref[0] = sg_ref[...]
        sh_out_ref[0] = sh_ref[...]


def _recur(u, l_valid, hist_a, hist_b, sg0, sh0, wa, wb, prm):
    bsz = sg0.shape[0]
    n = u.shape[0]
    n_sub = 1
    while n_sub < 4 and (n // bsz) % (2 * n_sub * CHUNK) == 0:
        n_sub *= 2
    rows = n_sub * CHUNK
    nc = n // bsz // rows
    consts = _recur_consts()
    cnames = ("lincl", "ucat", "incl", "strict", "eye", "bd")
    cvals = [jnp.asarray(consts[k]) for k in cnames]
    row_map = lambda col: (lambda b, ci: (b * nc + ci, col))
    full2 = lambda b, ci: (0, 0)
    per_b = lambda b, ci: (b, 0, 0)
    in_specs = [
        pl.BlockSpec((rows, 1024), row_map(0)),
        pl.BlockSpec((rows, 1024), row_map(1)),
        pl.BlockSpec((rows, 1024), row_map(2)),
        pl.BlockSpec((rows, 256), row_map(C_BB // 256)),
        pl.BlockSpec((1, HIST, 256), per_b),
        pl.BlockSpec((1, HIST, 768), per_b),
        pl.BlockSpec((1, 256, 256), per_b),
        pl.BlockSpec((1, 256, 256), per_b),
        pl.BlockSpec(wa.shape, full2),
        pl.BlockSpec(wb.shape, full2),
        pl.BlockSpec(prm.shape, full2),
    ] + [pl.BlockSpec(cv.shape, full2) for cv in cvals]
    out_shape = (jax.ShapeDtypeStruct((n, 768), F32),
                 jax.ShapeDtypeStruct((bsz, 256, 256), F32),
                 jax.ShapeDtypeStruct((bsz, 256, 256), F32),
                 jax.ShapeDtypeStruct((bsz, HIST, 256), F32))
    out_specs = (pl.BlockSpec((rows, 768), lambda b, ci: (b * nc + ci, 0)),
                 pl.BlockSpec((1, 256, 256), per_b),
                 pl.BlockSpec((1, 256, 256), per_b),
                 pl.BlockSpec((1, HIST, 256), per_b))
    scratch = [pltpu.VMEM((256, 256), F32), pltpu.VMEM((256, 256), F32),
               pltpu.VMEM((rows + HIST, 256), F32), pltpu.VMEM((rows + HIST, 768), F32),
               pltpu.VMEM((n_sub, CHUNK, 256), F32), pltpu.VMEM((n_sub, CHUNK, 256), F32),
               pltpu.VMEM((n_sub, SUB * CHUNK, 256), BF16)]
    assert n_sub == 1 or l_valid == CHUNK
    return pl.pallas_call(
        functools.partial(_recur_kernel, l_valid, n_sub),
        out_shape=out_shape,
        grid=(bsz, nc),
        in_specs=in_specs,
        out_specs=out_specs,
        scratch_shapes=scratch,
        compiler_params=pltpu.CompilerParams(
            dimension_semantics=("parallel", "arbitrary"), vmem_limit_bytes=VMEM_LIMIT),
        name="recur",
    )(u, u, u, u, hist_a, hist_b, sg0, sh0, wa, wb, prm, *cvals)


def _t5_bucket_np(dist):
    n = np.maximum(dist, 0)
    nf = np.maximum(n, MAX_EXACT).astype(np.float32)
    ratio = (np.log(nf / np.float32(MAX_EXACT)) / np.float32(math.log(MAX_DISTANCE / MAX_EXACT))).astype(np.float32)
    large = MAX_EXACT + (ratio * np.float32(N_BUCKETS - MAX_EXACT)).astype(np.int32)
    large = np.minimum(large, N_BUCKETS - 1)
    return np.where(n < MAX_EXACT, n, large).astype(np.int32)


def _bias_from_buckets(bkt, relb_ref, h):
    out = jnp.full(bkt.shape, NEG, F32)
    for b in range(N_BUCKETS):
        out = jnp.where(bkt == b, relb_ref[b, h] * LOG2E, out)
    return out


def _head_lane_mask(h, shape):
    lane = lax.broadcasted_iota(jnp.int32, shape, len(shape) - 1)
    return (lane >= h * HEAD_DIM) & (lane < (h + 1) * HEAD_DIM)


def _map_lane_mask(h, m, shape):
    lane = lax.broadcasted_iota(jnp.int32, shape, len(shape) - 1)
    lo = h * HEAD_DIM + m * HEAD_DIM_DQK
    return (lane >= lo) & (lane < lo + HEAD_DIM_DQK)


def _attn_finish(o1, o2, lam_ref, nd_ref, bd_bf):
    od = o1 - lam_ref[...] * o2
    ms = _segsum(od * od, bd_bf) * (1.0 / HEAD_DIM)
    return od * lax.rsqrt(ms + RMS_EPS) * nd_ref[...]


def _attn_prompt_kernel(qi_ref, kj_ref, relb_ref, q_ref, k_ref, v_ref, bkt_ref, lam_ref, nd_ref, bd_ref, o_ref,
                        q8_ref, m_ref, l_ref, acc_ref, bias_ref, s_ref, p_ref, al_ref):
    b = pl.program_id(0)
    step = pl.program_id(1)
    i = qi_ref[step]
    j = kj_ref[step]

    @pl.when((b == 0) & (step == 0))
    def _():
        for d in range(3):
            bkt = bkt_ref[d]
            for h in range(N_HEADS):
                bias_ref[d, h] = _bias_from_buckets(bkt, relb_ref, h)

    @pl.when(j == 0)
    def _():
        qt = (q_ref[...] * QK_SCALE_LOG2).T
        for hm in range(2 * N_HEADS):
            q8_ref[hm] = qt[hm * HEAD_DIM_DQK:(hm + 1) * HEAD_DIM_DQK, :].astype(BF16)
        m_ref[...] = jnp.full(m_ref.shape, NEG, F32)
        l_ref[...] = jnp.zeros(l_ref.shape, F32)
        acc_ref[...] = jnp.zeros(acc_ref.shape, F32)

    def update():
        vt = v_ref[...].T.astype(BF16)
        d = jnp.minimum(i - j, 2)
        for hm in range(2 * N_HEADS):
            k_hm = k_ref[:, hm * HEAD_DIM_DQK:(hm + 1) * HEAD_DIM_DQK].astype(BF16)
            s_ref[hm] = _dot(k_hm, q8_ref[hm]) + bias_ref[d, hm // 2]
        for hm in range(2 * N_HEADS):
            s = s_ref[hm]
            m_old = m_ref[hm]
            m_new = jnp.maximum(m_old, jnp.max(s, axis=0, keepdims=True))
            p = jnp.exp2(s - m_new)
            alpha = jnp.exp2(m_old - m_new)
            l_ref[hm] = alpha * l_ref[hm] + jnp.sum(p, axis=0, keepdims=True)
            p_ref[hm] = p.astype(BF16)
            al_ref[hm] = alpha
            m_ref[hm] = m_new
        for hm in range(2 * N_HEADS):
            h = hm // 2
            acc_ref[hm] = al_ref[hm] * acc_ref[hm] + _dot(vt[h * HEAD_DIM:(h + 1) * HEAD_DIM, :], p_ref[hm])

    update()

    @pl.when(j == i)
    def _():
        o1 = jnp.concatenate([acc_ref[2 * h] / l_ref[2 * h] for h in range(N_HEADS)], axis=0)
        o2 = jnp.concatenate([acc_ref[2 * h + 1] / l_ref[2 * h + 1] for h in range(N_HEADS)], axis=0)
        o_ref[...] = _attn_finish(o1.T, o2.T, lam_ref, nd_ref, bd_ref[...].astype(BF16))


def _attn_prompt(u, bsz, seq, rel_bias, lam_row, nd_row, bd):
    t = min(256, seq)
    nq = seq // t
    dist = [d * t + np.arange(t)[None, :] - np.arange(t)[:, None] for d in range(3)]
    bkt = np.stack([np.where(dd >= 0, _t5_bucket_np(dd), -1) for dd in dist]).astype(np.int32)
    assert nq <= 2 or t >= MAX_DISTANCE, "blocks two or more tiles back must share the last bucket"
    pairs = [(i, j) for i in range(nq) for j in range(i + 1)]
    qi = jnp.asarray(np.array([p[0] for p in pairs], np.int32))
    kj = jnp.asarray(np.array([p[1] for p in pairs], np.int32))
    full2 = lambda b, s, qi, kj: (0, 0)
    grid_spec = pltpu.PrefetchScalarGridSpec(
        num_scalar_prefetch=2,
        grid=(bsz, len(pairs)),
        in_specs=[
            pl.BlockSpec(memory_space=pltpu.SMEM),
            pl.BlockSpec((t, GROUP_W), lambda b, s, qi, kj: (b * nq + qi[s], C_DQ // 256)),
            pl.BlockSpec((t, GROUP_W), lambda b, s, qi, kj: (b * nq + kj[s], C_DK // 256)),
            pl.BlockSpec((t, GROUP_W), lambda b, s, qi, kj: (b * nq + kj[s], C_DV // 256)),
            pl.BlockSpec((3, t, t), lambda b, s, qi, kj: (0, 0, 0)),
            pl.BlockSpec((1, GROUP_W), full2),
            pl.BlockSpec((1, GROUP_W), full2),
            pl.BlockSpec((GROUP_W, GROUP_W), full2),
        ],
        out_specs=pl.BlockSpec((t, GROUP_W), lambda b, s, qi, kj: (b * nq + qi[s], 0)),
        scratch_shapes=[pltpu.VMEM((8, HEAD_DIM_DQK, t), BF16), pltpu.VMEM((8, 1, t), F32),
                        pltpu.VMEM((8, 1, t), F32), pltpu.VMEM((8, HEAD_DIM, t), F32),
                        pltpu.VMEM((3, N_HEADS, t, t), F32), pltpu.VMEM((8, t, t), F32),
                        pltpu.VMEM((8, t, t), BF16), pltpu.VMEM((8, 1, t), F32)],
    )
    return pl.pallas_call(
        _attn_prompt_kernel,
        out_shape=jax.ShapeDtypeStruct((bsz * seq, GROUP_W), F32),
        grid_spec=grid_spec,
        compiler_params=pltpu.CompilerParams(
            dimension_semantics=("arbitrary", "arbitrary"), vmem_limit_bytes=VMEM_LIMIT),
        name="attn_prompt",
    )(qi, kj, rel_bias, u, u, u, jnp.asarray(bkt), lam_row, nd_row, bd)


def _attn_sample_kernel(pp, n_steps, pt_ref, relb_ref, q_ref, kn_ref, vn_ref, *rest):
    k_refs = rest[:pp]
    v_refs = rest[pp:2 * pp]
    (bktl_ref, bktn_ref, lam_ref, nd_ref, o_ref,
     q16_ref, m_ref, l_ref, acc_ref, bias_ref, biasn_ref) = rest[2 * pp:]
    b = pl.program_id(0)
    s_id = pl.program_id(1)
    rows = 8
    page = k_refs[0].shape[-1]

    @pl.when((b == 0) & (s_id == 0))
    def _():
        for h in range(N_HEADS):
            near = _bias_from_buckets(bktl_ref[...], relb_ref, h)
            far = jnp.full((rows, page), relb_ref[N_BUCKETS - 1, h] * LOG2E, F32)
            newb = _bias_from_buckets(bktn_ref[...], relb_ref, h)
            for m in range(2):
                r0 = (2 * h + m) * rows
                for t in range(pp):
                    bias_ref[0, r0:r0 + rows, t * page:(t + 1) * page] = far
                    bias_ref[1, r0:r0 + rows, t * page:(t + 1) * page] = near if t == pp - 1 else far
                biasn_ref[r0:r0 + rows, :] = newb

    hrows = 2 * rows

    @pl.when(s_id == 0)
    def _():
        lane = lax.broadcasted_iota(jnp.int32, (rows, HEAD_DIM), 1)
        for h in range(N_HEADS):
            qh = q_ref[0, h] * QK_SCALE_LOG2
            q16_ref[h * hrows:h * hrows + rows, :] = jnp.where(lane < HEAD_DIM_DQK, qh, 0.0)
            q16_ref[h * hrows + rows:(h + 1) * hrows, :] = jnp.where(lane >= HEAD_DIM_DQK, qh, 0.0)
        s = jnp.concatenate(
            [_dot_nt(q16_ref[h * hrows:(h + 1) * hrows, :].astype(BF16), kn_ref[0, h].astype(BF16))
             for h in range(N_HEADS)], axis=0) + biasn_ref[...]
        m0 = jnp.max(s, axis=1, keepdims=True)
        p = jnp.exp2(s - m0)
        m_ref[...] = m0
        l_ref[...] = jnp.sum(p, axis=1, keepdims=True)
        for h in range(N_HEADS):
            acc_ref[h * hrows:(h + 1) * hrows, :] = _dot(
                p[h * hrows:(h + 1) * hrows, :].astype(BF16), vn_ref[0, h].astype(BF16))

    sel = jnp.where(s_id == n_steps - 1, 1, 0)
    s = jnp.concatenate(
        [_dot(q16_ref[h * hrows:(h + 1) * hrows, :].astype(BF16),
              jnp.concatenate([r[0, 0, h] for r in k_refs], axis=1).astype(BF16))
         for h in range(N_HEADS)], axis=0) + bias_ref[sel]
    m_old = m_ref[...]
    m_new = jnp.maximum(m_old, jnp.max(s, axis=1, keepdims=True))
    p32 = jnp.exp2(s - m_new)
    alpha = jnp.exp2(m_old - m_new)
    l_ref[...] = alpha * l_ref[...] + jnp.sum(p32, axis=1, keepdims=True)
    p = p32.astype(BF16)
    pv = jnp.concatenate(
        [_dot_nt(p[h * hrows:(h + 1) * hrows, :],
                 jnp.concatenate([r[0, 0, h] for r in v_refs], axis=1).astype(BF16))
         for h in range(N_HEADS)], axis=0)
    acc_ref[...] = alpha * acc_ref[...] + pv
    m_ref[...] = m_new

    @pl.when(s_id == n_steps - 1)
    def _():
        o = acc_ref[...] / l_ref[...]
        for h in range(N_HEADS):
            od = o[h * hrows:h * hrows + rows, :] - lam_ref[...] * o[h * hrows + rows:(h + 1) * hrows, :]
            ms = jnp.mean(od * od, axis=1, keepdims=True)
            o_ref[0, h] = od * lax.rsqrt(ms + RMS_EPS) * nd_ref[...]


def _attn_sample(layer, q4, k4, v4, ck, cv, page_table, l_new, rel_bias, lam_row, nd_row):
    bsz = q4.shape[0]
    n_pages = page_table.shape[1]
    page = ck.shape[-1]
    pp = 32
    while n_pages % pp:
        pp //= 2
    n_steps = n_pages // pp
    past = n_pages * page
    qpos = past + np.arange(8)[:, None]
    bkt_last = _t5_bucket_np(qpos - (past - page + np.arange(page))[None, :])
    dist_new = np.arange(8)[:, None] - np.arange(8)[None, :]
    ok = (dist_new >= 0) & (np.arange(8)[None, :] < l_new)
    bkt_new = np.where(ok, _t5_bucket_np(dist_new), -1).astype(np.int32)
    assert page >= MAX_DISTANCE, "only the last cache page may need distance-dependent bias"

    def page_map(t):
        return lambda b, s, pt: (layer, pt[b, s * pp + t], 0, 0, 0)

    per_b = lambda b, s, pt: (b, 0, 0, 0)
    full2 = lambda b, s, pt: (0, 0)
    new_spec = pl.BlockSpec((1, N_HEADS, 8, HEAD_DIM), per_b)
    page_spec = lambda t: pl.BlockSpec((1, 1, N_HEADS, HEAD_DIM, page), page_map(t))
    in_specs = ([pl.BlockSpec(memory_space=pltpu.SMEM), new_spec, new_spec, new_spec]
                + [page_spec(t) for t in range(pp)]
                + [page_spec(t) for t in range(pp)]
                + [pl.BlockSpec((8, page), full2), pl.BlockSpec((8, 8), full2),
                   pl.BlockSpec((1, HEAD_DIM), full2), pl.BlockSpec((1, HEAD_DIM), full2)])
    grid_spec = pltpu.PrefetchScalarGridSpec(
        num_scalar_prefetch=1,
        grid=(bsz, n_steps),
        in_specs=in_specs,
        out_specs=new_spec,
        scratch_shapes=[pltpu.VMEM((N_HEADS * 16, HEAD_DIM), F32), pltpu.VMEM((N_HEADS * 16, 1), F32),
                        pltpu.VMEM((N_HEADS * 16, 1), F32), pltpu.VMEM((N_HEADS * 16, HEAD_DIM), F32),
                        pltpu.VMEM((2, N_HEADS * 16, pp * page), F32), pltpu.VMEM((N_HEADS * 16, 8), F32)],
    )
    return pl.pallas_call(
        functools.partial(_attn_sample_kernel, pp, n_steps),
        out_shape=jax.ShapeDtypeStruct((bsz, N_HEADS, 8, HEAD_DIM), F32),
        grid_spec=grid_spec,
        compiler_params=pltpu.CompilerParams(
            dimension_semantics=("arbitrary", "arbitrary"), vmem_limit_bytes=VMEM_LIMIT),
        name="attn_sample",
    )(page_table, rel_bias, q4, k4, v4, *([ck] * pp), *([cv] * pp),
      jnp.asarray(bkt_last), jnp.asarray(bkt_new), lam_row[:, :HEAD_DIM], nd_row[:, :HEAD_DIM])


def _oproj_kernel(alpha, yabc_ref, yd_ref, x_ref, wo_ref, g_ref, b_ref, o_ref):
    mix = (_dot(yabc_ref[...].astype(BF16), wo_ref[0:768, :])
           + _dot(yd_ref[...].astype(BF16), wo_ref[768:1024, :]))
    o_ref[...] = _layer_norm(alpha * x_ref[...] + mix, g_ref[...], b_ref[...])


def _oproj(yabc, yd, x, wo, g, b, alpha, tm):
    n = x.shape[0]
    row = lambda i: (i, 0)
    full = lambda i: (0, 0)
    return pl.pallas_call(
        functools.partial(_oproj_kernel, alpha),
        out_shape=jax.ShapeDtypeStruct((n, D_MODEL), F32),
        grid=(n // tm,),
        in_specs=[pl.BlockSpec((tm, 768), row), pl.BlockSpec((tm, 256), row),
                  pl.BlockSpec((tm, D_MODEL), row), pl.BlockSpec((D_MODEL, D_MODEL), full),
                  pl.BlockSpec((1, D_MODEL), full), pl.BlockSpec((1, D_MODEL), full)],
        out_specs=pl.BlockSpec((tm, D_MODEL), row),
        compiler_params=pltpu.CompilerParams(
            dimension_semantics=("parallel",), vmem_limit_bytes=VMEM_LIMIT),
        name="out_proj_ln",
    )(yabc, yd, x, wo, g, b)


def _ffn_kernel(alpha, x_ref, wg_ref, wu_ref, wd_ref, g_ref, b_ref, o_ref, xb_ref, acc_ref):
    f = pl.program_id(1)

    @pl.when(f == 0)
    def _():
        xb_ref[...] = x_ref[...].astype(BF16)
        acc_ref[...] = jnp.zeros(acc_ref.shape, F32)

    xb = xb_ref[...]
    h = _silu(_dot(xb, wg_ref[...])) * _dot(xb, wu_ref[...])
    acc_ref[...] += _dot(h.astype(BF16), wd_ref[...])

    @pl.when(f == pl.num_programs(1) - 1)
    def _():
        o_ref[...] = _layer_norm(alpha * x_ref[...] + acc_ref[...], g_ref[...], b_ref[...])


def _ffn(x, wg, wu, wd, g, b, alpha, tm, tf):
    n = x.shape[0]
    ff = wg.shape[1]
    row = lambda i, f: (i, 0)
    full = lambda i, f: (0, 0)
    return pl.pallas_call(
        functools.partial(_ffn_kernel, alpha),
        out_shape=jax.ShapeDtypeStruct((n, D_MODEL), F32),
        grid=(n // tm, ff // tf),
        in_specs=[pl.BlockSpec((tm, D_MODEL), row),
                  pl.BlockSpec((D_MODEL, tf), lambda i, f: (0, f)),
                  pl.BlockSpec((D_MODEL, tf), lambda i, f: (0, f)),
                  pl.BlockSpec((tf, D_MODEL), lambda i, f: (f, 0)),
                  pl.BlockSpec((1, D_MODEL), full), pl.BlockSpec((1, D_MODEL), full)],
        out_specs=pl.BlockSpec((tm, D_MODEL), row),
        scratch_shapes=[pltpu.VMEM((tm, D_MODEL), BF16), pltpu.VMEM((tm, D_MODEL), F32)],
        compiler_params=pltpu.CompilerParams(
            dimension_semantics=("parallel", "arbitrary"), vmem_limit_bytes=VMEM_LIMIT),
        name="ffn_ln",
    )(x, wg, wu, wd, g, b)


def _moe_kernel(alpha, rb, x_ref, wrt_ref, wg_ref, wu_ref, wd_ref, g_ref, b_ref, o_ref,
                xb_ref, sel_ref, rank_ref, gate_ref, upper_ref, gall_ref, yh_ref, yl_ref):
    e = pl.program_id(1)
    tile = x_ref.shape[0]

    @pl.when(e == 0)
    def _():
        x = x_ref[...]
        xb_ref[...] = x.astype(BF16)
        o_ref[...] = jnp.zeros(o_ref.shape, F32)
        r_i = lax.broadcasted_iota(jnp.int32, (tile, tile), 0)
        c_i = lax.broadcasted_iota(jnp.int32, (tile, tile), 1)
        upper_ref[...] = jnp.where(r_i < c_i, 1.0, 0.0).astype(BF16)
        logits = lax.dot_general(wrt_ref[...], x, (((1,), (1,)), ((), ())),
                                 preferred_element_type=F32, precision=lax.Precision.HIGHEST)
        eid = lax.broadcasted_iota(jnp.int32, logits.shape, 0)
        logits = jnp.where(eid < N_EXPERTS, logits, NEG)
        big = logits.shape[0]
        m1 = jnp.max(logits, axis=0, keepdims=True)
        i1 = jnp.min(jnp.where(logits == m1, eid, big), axis=0, keepdims=True)
        rest = jnp.where(eid == i1, NEG, logits)
        m2 = jnp.max(rest, axis=0, keepdims=True)
        i2 = jnp.min(jnp.where(rest == m2, eid, big), axis=0, keepdims=True)
        e2 = jnp.exp(m2 - m1)
        g1 = 1.0 / (1.0 + e2)
        g2 = e2 / (1.0 + e2)
        sel = jnp.where((eid == i1) | (eid == i2), 1.0, 0.0)
        sel_ref[...] = sel
        gate_ref[...] = jnp.where(eid == i1, g1, 0.0) + jnp.where(eid == i2, g2, 0.0)
        rank_ref[...] = _dot(sel.astype(BF16), upper_ref[...])

    sel_row = sel_ref[pl.ds(e, 1), :]
    rank_row = rank_ref[pl.ds(e, 1), :]
    gate_row = gate_ref[pl.ds(e, 1), :]
    count = jnp.sum(sel_row).astype(jnp.int32)
    slot = lax.broadcasted_iota(jnp.int32, (rb, tile), 0).astype(F32)

    def expert_block(base):
        onehot = (slot + base == rank_row) & (sel_row > 0.5)
        gather = jnp.where(onehot, 1.0, 0.0).astype(BF16)
        xg = _dot(gather, xb_ref[...]).astype(BF16)
        h = _silu(_dot(xg, wg_ref[0])) * _dot(xg, wu_ref[0])
        y = _dot(h.astype(BF16), wd_ref[0])
        y = y * jnp.sum(jnp.where(onehot, gate_row, 0.0), axis=1, keepdims=True)
        return (gather,) + _split2(y)

    rows0 = pl.ds(pl.multiple_of(e * rb, 16), rb)
    gall_ref[rows0, :], yh_ref[rows0, :], yl_ref[rows0, :] = expert_block(0.0)

    def overflow(blk, carry):
        gather, yh, yl = expert_block((blk * rb).astype(F32))
        o_ref[...] += _dot_tn(gather, yh) + _dot_tn(gather, yl)
        return carry

    lax.fori_loop(1, (count + rb - 1) // rb, overflow, 0)

    @pl.when(e == pl.num_programs(1) - 1)
    def _():
        mix = o_ref[...] + _dot_tn(gall_ref[...], yh_ref[...]) + _dot_tn(gall_ref[...], yl_ref[...])
        o_ref[...] = _layer_norm(alpha * x_ref[...] + mix, g_ref[...], b_ref[...])


def _moe(x, wrt, wg, wu, wd, g, b, alpha, tile, rb):
    n = x.shape[0]
    n_e, _, ff = wg.shape
    row = lambda i, e: (i, 0)
    full = lambda i, e: (0, 0)
    per_e = lambda i, e: (e, 0, 0)
    return pl.pallas_call(
        functools.partial(_moe_kernel, alpha, rb),
        out_shape=jax.ShapeDtypeStruct((n, D_MODEL), F32),
        grid=(n // tile, n_e),
        in_specs=[pl.BlockSpec((tile, D_MODEL), row, pipeline_mode=pl.Buffered(1)),
                  pl.BlockSpec(wrt.shape, full),
                  pl.BlockSpec((1, D_MODEL, ff), per_e),
                  pl.BlockSpec((1, D_MODEL, ff), per_e),
                  pl.BlockSpec((1, ff, D_MODEL), per_e),
                  pl.BlockSpec((1, D_MODEL), full), pl.BlockSpec((1, D_MODEL), full)],
        out_specs=pl.BlockSpec((tile, D_MODEL), row),
        scratch_shapes=[pltpu.VMEM((tile, D_MODEL), BF16), pltpu.VMEM((16, tile), F32),
                        pltpu.VMEM((16, tile), F32), pltpu.VMEM((16, tile), F32),
                        pltpu.VMEM((tile, tile), BF16), pltpu.VMEM((n_e * rb, tile), BF16),
                        pltpu.VMEM((n_e * rb, D_MODEL), BF16), pltpu.VMEM((n_e * rb, D_MODEL), BF16)],
        compiler_params=pltpu.CompilerParams(
            dimension_semantics=("parallel", "arbitrary"), vmem_limit_bytes=MOE_VMEM_LIMIT),
        name="moe_ln",
    )(x, wrt, wg, wu, wd, g, b)


def _w_in_columns():
    o = {}
    off = 0
    for name, size in (("a_in", 256), ("a_gb", 256), ("a_gc", 256), ("b_qkv", 768), ("b_a", 4), ("b_b", 4),
                       ("b_z", 256), ("c_q", 256), ("c_f", 256), ("c_i", 256), ("c_g", 256),
                       ("d_q", 256), ("d_k", 256), ("d_v", 256)):
        o[name] = np.arange(off, off + size)
        off += size
    rep = lambda ix: np.repeat(ix, HEAD_DIM)
    cols = np.concatenate([o["a_in"], o["a_gb"], o["a_gc"], o["b_z"],
                           o["b_qkv"], rep(o["b_a"]),
                           o["c_q"], o["c_f"], o["c_i"], o["c_g"],
                           rep(o["b_b"]), o["d_q"], o["d_k"], o["d_v"]])
    assert cols.shape[0] == U_W
    return cols


def _pick_tile(n, pref):
    t = min(pref, n)
    while n % t:
        t //= 2
    return t


def _blockdiag_state(s):
    bsz = s.shape[0]
    eye = jnp.eye(N_HEADS, dtype=s.dtype)
    return (s[:, :, :, None, :] * eye[None, :, None, :, None]).reshape(bsz, GROUP_W, GROUP_W)


def _diag_blocks(s):
    bsz = s.shape[0]
    s5 = s.reshape(bsz, N_HEADS, HEAD_DIM, N_HEADS, HEAD_DIM)
    return jnp.stack([s5[:, h, :, h, :] for h in range(N_HEADS)], axis=1)


def _run_group(x, init, attend, lw, depth):
    bsz, seq, _ = x.shape
    lp = -(-seq // CHUNK) * CHUNK
    assert (lp == seq or seq < CHUNK) and seq >= CONV_B_W - 1
    n = bsz * seq
    alpha = (2 * depth) ** 0.25
    xf = x.reshape(n, D_MODEL)
    tm = _pick_tile(n, 512)
    outs = []
    for l in range(depth):
        w = lw[l]
        u = _proj(xf, w["w_in"], tm, 1024)
        u3 = u.reshape(bsz, seq, U_W)
        if init is None:
            hist_a = jnp.zeros((bsz, HIST, 256), F32)
            hist_b = jnp.zeros((bsz, HIST, 768), F32)
            sg0 = jnp.zeros((bsz, GROUP_W, GROUP_W), F32)
            sh0 = sg0
        else:
            ca, cb, s_g, s_h = init[l]
            hist_a = jnp.pad(ca, ((0, 0), (HIST - ca.shape[1], 0), (0, 0)))
            hist_b = jnp.pad(cb, ((0, 0), (HIST - cb.shape[1], 0), (0, 0)))
            sg0 = _blockdiag_state(s_g)
            sh0 = _blockdiag_state(jnp.swapaxes(s_h, 2, 3))
        u_rec = u if lp == seq else jnp.pad(u3, ((0, 0), (0, lp - seq), (0, 0))).reshape(bsz * lp, U_W)
        yabc, sg1, sh1, tail_a = _recur(u_rec, min(seq, CHUNK) if lp != seq else CHUNK,
                                        hist_a, hist_b, sg0, sh0, w["conv_a"], w["conv_b"], w["prm"])
        if lp != seq:
            yabc = yabc.reshape(bsz, lp, 768)[:, :seq].reshape(n, 768)
        yd = attend(l, u, u3, w)
        x1 = _oproj(yabc, yd, xf, w["w_o"], w["ln1_g"], w["ln1_b"], alpha, tm)
        f = w["ffn"]
        if f["moe"]:
            tile = _pick_tile(n, 1024)
            rb = -(-(tile * 9 // 32) // 16) * 16
            xf = _moe(x1, f["wrt"], f["wg"], f["wu"], f["wd"], w["ln2_g"], w["ln2_b"], alpha, tile, rb)
        else:
            xf = _ffn(x1, f["wg"], f["wu"], f["wd"], w["ln2_g"], w["ln2_b"], alpha,
                      _pick_tile(n, 512), f["tf"])
        a0 = HIST - (CONV_A_W - 1)
        outs.append((
            u3[:, :, C_DK:C_DK + 256].reshape(bsz, seq, N_HEADS, HEAD_DIM),
            u3[:, :, C_DV:C_DV + 256].reshape(bsz, seq, N_HEADS, HEAD_DIM),
            tail_a[:, a0:a0 + CONV_A_W - 1],
            u3[:, seq - (CONV_B_W - 1):, C_BQKV:C_BQKV + 768],
            _diag_blocks(sg1),
            jnp.swapaxes(_diag_blocks(sh1), 2, 3),
        ))
    return xf.reshape(bsz, seq, D_MODEL), [jnp.stack([o[i] for o in outs]) for i in range(6)]


def kernel(x_prompt, x_sample, cache_k, cache_v, state_conv_a, state_conv_b, state_gdn, state_hgrn, page_table, w_in, conv_a, conv_b, gdn_a_log, gdn_dt_bias, norm_b, lower_bounds, norm_c, lambda_q1, lambda_k1, lambda_q2, lambda_k2, norm_d, rel_bias, w_o, ln1_g, ln1_b, ffn_w_gate, ffn_w_up, ffn_w_down, router_w, moe_w_gate, moe_w_up, moe_w_down, ln2_g, ln2_b):
    depth = w_in.shape[0]
    cols = _w_in_columns()
    rep = lambda t: jnp.repeat(t.astype(F32), HEAD_DIM)
    tile4 = lambda t: jnp.tile(t.astype(F32), N_HEADS)
    lbs = jax.nn.softmax(lower_bounds.astype(F32), axis=0)
    lb_all = jnp.cumsum(lbs, axis=0) - lbs[0]
    bd = jnp.asarray(_recur_consts()["bd"])
    lw = []
    for l in range(depth):
        lam_init = 0.8 - 0.6 * math.exp(-0.3 * l)
        lam = (jnp.exp(jnp.sum(lambda_q1[l].astype(F32) * lambda_k1[l].astype(F32)))
               - jnp.exp(jnp.sum(lambda_q2[l].astype(F32) * lambda_k2[l].astype(F32))) + lam_init)
        prm = jnp.stack([-jnp.exp(rep(gdn_a_log[l])) * LOG2E, rep(gdn_dt_bias[l]), tile4(norm_b[l]),
                         tile4(norm_c[l]), lb_all[l], jnp.zeros((GROUP_W,), F32),
                         jnp.zeros((GROUP_W,), F32), jnp.zeros((GROUP_W,), F32)])
        j = l // 2
        if l % 2 == 0:
            ff = ffn_w_gate.shape[2]
            ffn = dict(moe=False, wg=ffn_w_gate[j].astype(BF16), wu=ffn_w_up[j].astype(BF16),
                       wd=ffn_w_down[j].astype(BF16))
        else:
            ff = moe_w_gate.shape[3]
            ffn = dict(moe=True, wrt=jnp.pad(router_w[j].astype(F32).T, ((0, 16 - N_EXPERTS), (0, 0))),
                       wg=moe_w_gate[j].astype(BF16), wu=moe_w_up[j].astype(BF16),
                       wd=moe_w_down[j].astype(BF16))
        tf = ff
        for cand in (1408, 1024, 512, 256, 128):
            if ff % cand == 0:
                tf = cand
                break
        ffn["tf"] = tf
        lw.append(dict(
            w_in=w_in[l][:, cols].astype(BF16),
            conv_a=jnp.pad(conv_a[l].astype(F32), ((0, HIST - CONV_A_W), (0, 0))),
            conv_b=jnp.pad(conv_b[l].astype(F32), ((0, HIST - CONV_B_W), (0, 0))),
            prm=prm,
            lam_row=jnp.full((1, GROUP_W), lam, F32),
            nd_row=(tile4(norm_d[l]) * (1.0 - lam_init))[None, :],
            w_o=w_o[l].astype(BF16),
            ln1_g=ln1_g[l][None].astype(F32), ln1_b=ln1_b[l][None].astype(F32),
            ln2_g=ln2_g[l][None].astype(F32), ln2_b=ln2_b[l][None].astype(F32),
            ffn=ffn,
        ))
    relb = rel_bias.astype(F32)

    def attend_prompt(l, u, u3, w):
        bsz, seq, _ = u3.shape
        return _attn_prompt(u, bsz, seq, relb, w["lam_row"], w["nd_row"], bd)

    ck_t = jnp.transpose(cache_k, (0, 1, 3, 4, 2))
    cv_t = jnp.transpose(cache_v, (0, 1, 3, 4, 2))

    def attend_sample(l, u, u3, w):
        bsz, seq, _ = u3.shape

        def heads8(c0):
            t = u3[:, :, c0:c0 + 256].reshape(bsz, seq, N_HEADS, HEAD_DIM)
            return jnp.pad(jnp.swapaxes(t, 1, 2), ((0, 0), (0, 0), (0, 8 - seq), (0, 0)))

        o = _attn_sample(l, heads8(C_DQ), heads8(C_DK), heads8(C_DV), ck_t, cv_t,
                         page_table, seq, relb, w["lam_row"], w["nd_row"])
        return jnp.swapaxes(o[:, :, :seq], 1, 2).reshape(bsz * seq, GROUP_W)

    init_s = [(state_conv_a[l], state_conv_b[l], state_gdn[l], state_hgrn[l]) for l in range(depth)]
    y_p, (k_p, v_p, ca_p, cb_p, sg_p, sh_p) = _run_group(x_prompt, None, attend_prompt, lw, depth)
    y_s, (k_s, v_s, ca_s, cb_s, sg_s, sh_s) = _run_group(x_sample, init_s, attend_sample, lw, depth)
    return (y_p, y_s, k_p, v_p, k_s, v_s, ca_p, ca_s, cb_p, cb_s, sg_p, sg_s, sh_p, sh_s)
```
